```python
import jax, jax.numpy as jnp
from jax import lax
import numpy as np

D_MODEL = 1024
BATCH = 4
SEQ = 8192
DEPTH = 1

CONV_WIDTH = 512
CONV_GROUPS = 8
CONV_KERNEL = 3
N_Q_HEADS = 8
N_KV_HEADS = 2
HEAD_DIM = 64
ATTN_WIDTH = N_Q_HEADS * HEAD_DIM
KV_WIDTH = N_KV_HEADS * HEAD_DIM
WINDOW = 128
BLOCK = 128
ROPE_THETA = 500000.0
ROT_DIM = HEAD_DIM // 4
MIX_WIDTH = CONV_WIDTH + ATTN_WIDTH
IN_PROJ_WIDTH = 3 * CONV_WIDTH + ATTN_WIDTH + 2 * KV_WIDTH
D_FF = 2816
FFN_RES_SCALE = 0.5
RMS_EPS = 1e-5
MASK_VALUE = -1e30

kernel_name = "hybrid_shortconv_swa_sink_macaron"


def rms_norm(x, gain):
    xf = x.astype(jnp.float32)
    inv = lax.rsqrt(jnp.mean(xf * xf, axis=-1, keepdims=True) + RMS_EPS)
    return (xf * inv).astype(x.dtype) * gain


def swiglu(h, w_gate, w_up, w_down):
    return (jax.nn.silu(h @ w_gate) * (h @ w_up)) @ w_down


def partial_rotary(t, seq_len):
    half = ROT_DIM // 2
    inv_freq = ROPE_THETA ** (-jnp.arange(0, ROT_DIM, 2, dtype=jnp.float32) / ROT_DIM)
    ang = jnp.arange(seq_len, dtype=jnp.float32)[:, None] * inv_freq[None, :]
    cos = jnp.cos(ang)[None, :, None, :].astype(t.dtype)
    sin = jnp.sin(ang)[None, :, None, :].astype(t.dtype)
    t1, t2, t_pass = t[..., :half], t[..., half:ROT_DIM], t[..., ROT_DIM:]
    return jnp.concatenate([t1 * cos - t2 * sin, t2 * cos + t1 * sin, t_pass], axis=-1)


def short_conv_mixer(b_gate, c_gate, u, conv_w):
    v = c_gate * u
    y = lax.conv_general_dilated(
        v, conv_w[:, None, :], window_strides=(1,), padding=[(CONV_KERNEL - 1, 0)],
        dimension_numbers=('NWC', 'WIO', 'NWC'), feature_group_count=CONV_WIDTH)
    return b_gate * y


def sliding_window_sink_attention(q, k, v, sinks):
    b, s = q.shape[0], q.shape[1]
    nb = s // BLOCK
    g = N_Q_HEADS // N_KV_HEADS
    qb = q.reshape(b, nb, BLOCK, N_KV_HEADS, g, HEAD_DIM)

    def band(t):
        tp = jnp.pad(t, ((0, 0), (BLOCK, 0), (0, 0), (0, 0))).reshape(b, nb + 1, BLOCK, N_KV_HEADS, HEAD_DIM)
        return jnp.concatenate([tp[:, :-1], tp[:, 1:]], axis=2)

    kb, vb = band(k), band(v)
    scores = jnp.einsum('bnqhgd,bnkhd->bnhgqk', qb, kb).astype(jnp.float32) * (HEAD_DIM ** -0.5)

    qi = jnp.arange(BLOCK)[:, None]
    kj = jnp.arange(2 * BLOCK)[None, :]
    rel = kj - BLOCK - qi
    in_window = (rel <= 0) & (rel > -WINDOW)
    blk = jnp.arange(nb)[:, None, None]
    k_exists = (blk * BLOCK + kj[None] - BLOCK) >= 0
    mask = (in_window[None] & k_exists)[None, :, None, None]
    scores = jnp.where(mask, scores, MASK_VALUE)

    sink = sinks.astype(jnp.float32).reshape(N_KV_HEADS, g)[None, None, :, :, None, None]
    m = jnp.maximum(jnp.max(scores, axis=-1, keepdims=True), sink)
    p = jnp.exp(scores - m)
    probs = p / (jnp.sum(p, axis=-1, keepdims=True) + jnp.exp(sink - m))
    out = jnp.einsum('bnhgqk,bnkhd->bnqhgd', probs.astype(vb.dtype), vb)
    return out.reshape(b, s, ATTN_WIDTH)


def setup_inputs(seed: int = 0) -> dict:
    key = jax.random.key(seed)
    ks = jax.random.split(key, 16)
    f32 = jnp.float32

    def w(k, shape, fan_in):
        return jax.random.normal(k, shape, f32) * (fan_in ** -0.5)

    def gain(k):
        return 1.0 + 0.02 * jax.random.normal(k, (DEPTH, D_MODEL), f32)

    return {
        "x": jax.random.normal(ks[0], (BATCH, SEQ, D_MODEL), f32),
        "ffn1_norm": gain(ks[1]),
        "ffn1_w_gate": w(ks[2], (DEPTH, D_MODEL, D_FF), D_MODEL),
        "ffn1_w_up": w(ks[3], (DEPTH, D_MODEL, D_FF), D_MODEL),
        "ffn1_w_down": w(ks[4], (DEPTH, D_FF, D_MODEL), D_FF),
        "mix_norm": gain(ks[5]),
        "w_in": w(ks[6], (DEPTH, D_MODEL, IN_PROJ_WIDTH), D_MODEL),
        "conv_w": w(ks[7], (DEPTH, CONV_KERNEL, CONV_WIDTH), CONV_KERNEL),
        "attn_sinks": 0.5 * jax.random.normal(ks[8], (DEPTH, N_Q_HEADS), f32),
        "w_out": w(ks[9], (DEPTH, MIX_WIDTH, D_MODEL), MIX_WIDTH),
        "ffn2_norm": gain(ks[10]),
        "ffn2_w_gate": w(ks[11], (DEPTH, D_MODEL, D_FF), D_MODEL),
        "ffn2_w_up": w(ks[12], (DEPTH, D_MODEL, D_FF), D_MODEL),
        "ffn2_w_down": w(ks[13], (DEPTH, D_FF, D_MODEL), D_FF),
        "final_norm": 1.0 + 0.02 * jax.random.normal(ks[14], (D_MODEL,), f32),
    }


def reference(x, ffn1_norm, ffn1_w_gate, ffn1_w_up, ffn1_w_down, mix_norm, w_in, conv_w,
              attn_sinks, w_out, ffn2_norm, ffn2_w_gate, ffn2_w_up, ffn2_w_down, final_norm):
    b, s, _ = x.shape
    splits = np.cumsum([CONV_WIDTH, CONV_WIDTH, CONV_WIDTH, ATTN_WIDTH, KV_WIDTH]).tolist()
    for l in range(DEPTH):
        x = x + FFN_RES_SCALE * swiglu(rms_norm(x, ffn1_norm[l]), ffn1_w_gate[l], ffn1_w_up[l], ffn1_w_down[l])

        h = rms_norm(x, mix_norm[l])
        z = h @ w_in[l]
        b_gate, c_gate, u, q, k, v = jnp.split(z, splits, axis=-1)

        y_conv = short_conv_mixer(b_gate, c_gate, u, conv_w[l])

        q = partial_rotary(q.reshape(b, s, N_Q_HEADS, HEAD_DIM), s)
        k = partial_rotary(k.reshape(b, s, N_KV_HEADS, HEAD_DIM), s)
        v = v.reshape(b, s, N_KV_HEADS, HEAD_DIM)
        y_attn = sliding_window_sink_attention(q, k, v, attn_sinks[l])

        x = x + jnp.concatenate([y_conv, y_attn], axis=-1) @ w_out[l]

        x = x + FFN_RES_SCALE * swiglu(rms_norm(x, ffn2_norm[l]), ffn2_w_gate[l], ffn2_w_up[l], ffn2_w_down[l])
    return rms_norm(x, final_norm)
```

```python
import functools

import jax
import jax.numpy as jnp
from jax import lax
from jax.experimental import pallas as pl
from jax.experimental.pallas import tpu as pltpu

D_MODEL = 1024
D_FF = 2816
CONV_WIDTH = 512
CONV_KERNEL = 3
N_Q_HEADS = 8
N_KV_HEADS = 2
Q_PER_KV = N_Q_HEADS // N_KV_HEADS
HEAD_DIM = 64
ATTN_WIDTH = N_Q_HEADS * HEAD_DIM
KV_WIDTH = N_KV_HEADS * HEAD_DIM
WINDOW = 128
BLOCK = 128
ROPE_THETA = 500000.0
ROT_DIM = HEAD_DIM // 4
ROT_HALF = ROT_DIM // 2
MIX_WIDTH = CONV_WIDTH + ATTN_WIDTH
IN_PROJ_WIDTH = 3 * CONV_WIDTH + ATTN_WIDTH + 2 * KV_WIDTH
FFN_RES_SCALE = 0.5
RMS_EPS = 1e-5
MASK_VALUE = -1e30

LANES = 128
BF16_SUBLANES = 16
FFN_TILE = 512
MIX_TILE = 512
FFN_CHUNK = 1024
VMEM_LIMIT_BYTES = 56 * 1024 * 1024

_F32 = jnp.float32
_BF16 = jnp.bfloat16


def _rms_norm(x, gain):
    inv = lax.rsqrt(jnp.mean(x * x, axis=-1, keepdims=True) + RMS_EPS)
    return (x * inv) * gain


def _swiglu_residual(x, gain, wg_ref, wu_ref, wd_ref):
    h = _rms_norm(x, gain).astype(_BF16)
    acc = None
    for c0 in range(0, D_FF, FFN_CHUNK):
        c1 = min(c0 + FFN_CHUNK, D_FF)
        g = jnp.dot(h, wg_ref[:, c0:c1], preferred_element_type=_F32)
        u = jnp.dot(h, wu_ref[:, c0:c1], preferred_element_type=_F32)
        a = (g * jax.nn.sigmoid(g) * u).astype(_BF16)
        part = jnp.dot(a, wd_ref[c0:c1, :], preferred_element_type=_F32)
        acc = part if acc is None else acc + part
    return x + FFN_RES_SCALE * acc


def _rotate_head_pair(t, cos, sin_lo, sin_hi):
    return t * cos + pltpu.roll(t, LANES - ROT_HALF, 1) * sin_lo + pltpu.roll(t, ROT_HALF, 1) * sin_hi


def _ffn_inproj_kernel(x_ref, g1_ref, wg_ref, wu_ref, wd_ref, g2_ref, win_ref, rot_ref,
                       x1_ref, bg_ref, cu_ref, q_ref, kv_ref):
    x1 = _swiglu_residual(x_ref[...], g1_ref[...], wg_ref, wu_ref, wd_ref)
    x1_ref[...] = x1
    h = _rms_norm(x1, g2_ref[...]).astype(_BF16)
    z = jnp.dot(h, win_ref[...], preferred_element_type=_F32)
    bg_ref[...] = z[:, :CONV_WIDTH].astype(_BF16)
    cu_ref[...] = (z[:, CONV_WIDTH:2 * CONV_WIDTH] * z[:, 2 * CONV_WIDTH:3 * CONV_WIDTH]).astype(_BF16)
    cos = rot_ref[:, :LANES]
    sin_lo = rot_ref[:, LANES:2 * LANES]
    sin_hi = rot_ref[:, 2 * LANES:]
    q0 = 3 * CONV_WIDTH
    for j in range(ATTN_WIDTH // LANES):
        t = z[:, q0 + j * LANES:q0 + (j + 1) * LANES]
        q_ref[:, j * LANES:(j + 1) * LANES] = (
            _rotate_head_pair(t, cos, sin_lo, sin_hi) * (HEAD_DIM ** -0.5)).astype(_BF16)
    k0 = q0 + ATTN_WIDTH
    kv_ref[:, :KV_WIDTH] = _rotate_head_pair(z[:, k0:k0 + KV_WIDTH], cos, sin_lo, sin_hi).astype(_BF16)
    kv_ref[:, KV_WIDTH:] = z[:, k0 + KV_WIDTH:].astype(_BF16)


def _ffn_final_kernel(x_ref, g_ref, wg_ref, wu_ref, wd_ref, gf_ref, out_ref):
    x = _swiglu_residual(x_ref[...], g_ref[...], wg_ref, wu_ref, wd_ref)
    out_ref[...] = _rms_norm(x, gf_ref[...])


def _mixer_kernel(sinks_ref, x1_ref, bg_ref, cu_ref, cu_halo_ref, q_ref, kv_ref, kv_halo_ref,
                  convw_ref, wout_ref, x2_ref):
    first_tile = pl.program_id(1) == 0
    rows = x1_ref.shape[0]

    v = cu_ref[...].astype(_F32)
    halo = jnp.where(first_tile, 0.0, cu_halo_ref[...].astype(_F32))
    prev1 = halo[BF16_SUBLANES - 1:BF16_SUBLANES, :]
    prev2 = halo[BF16_SUBLANES - 2:BF16_SUBLANES - 1, :]
    row = lax.broadcasted_iota(jnp.int32, (rows, 1), 0)
    v1 = jnp.where(row == 0, prev1, pltpu.roll(v, 1, 0))
    v2 = jnp.where(row == 0, prev2, jnp.where(row == 1, prev1, pltpu.roll(v, 2, 0)))
    w = convw_ref[...]
    y_conv = bg_ref[...].astype(_F32) * (w[0:1, :] * v2 + w[1:2, :] * v1 + w[2:3, :] * v)

    kv_ext = jnp.concatenate([kv_halo_ref[...], kv_ref[...]], axis=0)
    qi = lax.broadcasted_iota(jnp.int32, (BLOCK, 2 * BLOCK), 0)
    kj = lax.broadcasted_iota(jnp.int32, (BLOCK, 2 * BLOCK), 1)
    in_window = (kj > qi) & (kj <= qi + BLOCK)
    attn_blocks = []
    for blk in range(rows // BLOCK):
        r0 = blk * BLOCK
        mask = in_window if blk > 0 else in_window & ((kj >= BLOCK) | jnp.logical_not(first_tile))
        heads = []
        for hkv in range(N_KV_HEADS):
            k = kv_ext[r0:r0 + 2 * BLOCK, hkv * HEAD_DIM:(hkv + 1) * HEAD_DIM]
            vv = kv_ext[r0:r0 + 2 * BLOCK, KV_WIDTH + hkv * HEAD_DIM:KV_WIDTH + (hkv + 1) * HEAD_DIM]
            for g in range(Q_PER_KV):
                hq = hkv * Q_PER_KV + g
                q = q_ref[r0:r0 + BLOCK, hq * HEAD_DIM:(hq + 1) * HEAD_DIM]
                s = lax.dot_general(q, k, (((1,), (1,)), ((), ())), preferred_element_type=_F32)
                s = jnp.where(mask, s, MASK_VALUE)
                sink = sinks_ref[hq]
                m = jnp.maximum(jnp.max(s, axis=-1, keepdims=True), sink)
                p = jnp.exp(s - m)
                denom = jnp.sum(p, axis=-1, keepdims=True) + jnp.exp(sink - m)
                heads.append(jnp.dot(p.astype(_BF16), vv, preferred_element_type=_F32) / denom)
        attn_blocks.append(jnp.concatenate(heads, axis=1))
    y_attn = jnp.concatenate(attn_blocks, axis=0)

    y = jnp.concatenate([y_conv, y_attn], axis=1).astype(_BF16)
    x2_ref[...] = x1_ref[...] + jnp.dot(y, wout_ref[...], preferred_element_type=_F32)


def _resident(shape):
    return pl.BlockSpec(shape, lambda *_: (0,) * len(shape), pipeline_mode=pl.Buffered(1))


def _rotary_table(seq_len):
    inv_freq = ROPE_THETA ** (-jnp.arange(0, ROT_DIM, 2, dtype=_F32) / ROT_DIM)
    ang = jnp.arange(seq_len, dtype=_F32)[:, None] * inv_freq[None, :]
    cos, sin = jnp.cos(ang), jnp.sin(ang)
    ones = jnp.ones((seq_len, HEAD_DIM - ROT_DIM), _F32)
    zeros = jnp.zeros((seq_len, HEAD_DIM - ROT_HALF), _F32)
    cos_head = jnp.concatenate([cos, cos, ones], axis=1)
    lo_head = jnp.concatenate([-sin, zeros], axis=1)
    hi_head = jnp.concatenate([zeros[:, :ROT_HALF], sin, zeros[:, ROT_HALF:]], axis=1)
    return jnp.concatenate([cos_head, cos_head, lo_head, lo_head, hi_head, hi_head], axis=1)


def _row_spec(tile, width):
    return pl.BlockSpec((tile, width), lambda i: (i, 0))


_FFN_WEIGHT_SPECS = [_resident((D_MODEL, D_FF)), _resident((D_MODEL, D_FF)), _resident((D_FF, D_MODEL))]
_PARAMS = functools.partial(pltpu.CompilerParams, vmem_limit_bytes=VMEM_LIMIT_BYTES)


def _ffn_inproj_call(xt, gain1, wg, wu, wd, gain2, win, rot_table):
    tokens = xt.shape[0]
    seq_tiles = rot_table.shape[0] // FFN_TILE
    return pl.pallas_call(
        _ffn_inproj_kernel,
        grid=(tokens // FFN_TILE,),
        in_specs=[_row_spec(FFN_TILE, D_MODEL), _resident((1, D_MODEL)), *_FFN_WEIGHT_SPECS,
                  _resident((1, D_MODEL)), _resident((D_MODEL, IN_PROJ_WIDTH)),
                  pl.BlockSpec((FFN_TILE, 3 * LANES), lambda i: (i % seq_tiles, 0))],
        out_specs=[_row_spec(FFN_TILE, D_MODEL), _row_spec(FFN_TILE, CONV_WIDTH), _row_spec(FFN_TILE, CONV_WIDTH),
                   _row_spec(FFN_TILE, ATTN_WIDTH), _row_spec(FFN_TILE, 2 * KV_WIDTH)],
        out_shape=[jax.ShapeDtypeStruct((tokens, D_MODEL), _F32),
                   jax.ShapeDtypeStruct((tokens, CONV_WIDTH), _BF16),
                   jax.ShapeDtypeStruct((tokens, CONV_WIDTH), _BF16),
                   jax.ShapeDtypeStruct((tokens, ATTN_WIDTH), _BF16),
                   jax.ShapeDtypeStruct((tokens, 2 * KV_WIDTH), _BF16)],
        compiler_params=_PARAMS(dimension_semantics=("parallel",)),
        name="ffn1_inproj",
    )(xt, gain1, wg, wu, wd, gain2, win, rot_table)


def _mixer_call(sinks, x1, bg, cu, q, kv, conv_w, wout, batch):
    tokens = x1.shape[0]
    mix_tiles = tokens // batch // MIX_TILE
    halo_per_tile = MIX_TILE // BF16_SUBLANES
    blocks_per_tile = MIX_TILE // BLOCK

    def tile_spec(width):
        return pl.BlockSpec((MIX_TILE, width), lambda b, i, sinks: (b * mix_tiles + i, 0))

    return pl.pallas_call(
        _mixer_kernel,
        grid_spec=pltpu.PrefetchScalarGridSpec(
            num_scalar_prefetch=1,
            grid=(batch, mix_tiles),
            in_specs=[
                tile_spec(D_MODEL), tile_spec(CONV_WIDTH), tile_spec(CONV_WIDTH),
                pl.BlockSpec((BF16_SUBLANES, CONV_WIDTH),
                             lambda b, i, sinks: (jnp.maximum((b * mix_tiles + i) * halo_per_tile - 1, 0), 0)),
                tile_spec(ATTN_WIDTH), tile_spec(2 * KV_WIDTH),
                pl.BlockSpec((BLOCK, 2 * KV_WIDTH),
                             lambda b, i, sinks: (jnp.maximum((b * mix_tiles + i) * blocks_per_tile - 1, 0), 0)),
                pl.BlockSpec((CONV_KERNEL, CONV_WIDTH), lambda b, i, sinks: (0, 0), pipeline_mode=pl.Buffered(1)),
                pl.BlockSpec((MIX_WIDTH, D_MODEL), lambda b, i, sinks: (0, 0), pipeline_mode=pl.Buffered(1)),
            ],
            out_specs=tile_spec(D_MODEL),
        ),
        out_shape=jax.ShapeDtypeStruct((tokens, D_MODEL), _F32),
        compiler_params=_PARAMS(dimension_semantics=("parallel", "parallel")),
        name="mixer_outproj",
    )(sinks, x1, bg, cu, cu, q, kv, kv, conv_w, wout)


def _ffn_final_call(x2, gain, wg, wu, wd, final_gain):
    tokens = x2.shape[0]
    return pl.pallas_call(
        _ffn_final_kernel,
        grid=(tokens // FFN_TILE,),
        in_specs=[_row_spec(FFN_TILE, D_MODEL), _resident((1, D_MODEL)), *_FFN_WEIGHT_SPECS,
                  _resident((1, D_MODEL))],
        out_specs=_row_spec(FFN_TILE, D_MODEL),
        out_shape=jax.ShapeDtypeStruct((tokens, D_MODEL), _F32),
        compiler_params=_PARAMS(dimension_semantics=("parallel",)),
        name="ffn2_final",
    )(x2, gain, wg, wu, wd, final_gain)


def kernel(x, ffn1_norm, ffn1_w_gate, ffn1_w_up, ffn1_w_down, mix_norm, w_in, conv_w, attn_sinks, w_out,
           ffn2_norm, ffn2_w_gate, ffn2_w_up, ffn2_w_down, final_norm):
    batch, seq, d_model = x.shape
    assert d_model == D_MODEL and ffn1_norm.shape[0] == 1
    assert seq % MIX_TILE == 0 and seq % FFN_TILE == 0
    xt = x.reshape(batch * seq, D_MODEL)
    x1, bg, cu, q, kv = _ffn_inproj_call(
        xt, ffn1_norm, ffn1_w_gate[0].astype(_BF16), ffn1_w_up[0].astype(_BF16), ffn1_w_down[0].astype(_BF16),
        mix_norm, w_in[0].astype(_BF16), _rotary_table(seq))
    x2 = _mixer_call(attn_sinks[0], x1, bg, cu, q, kv, conv_w[0], w_out[0].astype(_BF16), batch)
    out = _ffn_final_call(
        x2, ffn2_norm, ffn2_w_gate[0].astype(_BF16), ffn2_w_up[0].astype(_BF16), ffn2_w_down[0].astype(_BF16),
        final_norm.reshape(1, D_MODEL))
    return out.reshape(batch, seq, D_MODEL)
```

```python
import functools

import jax
import jax.numpy as jnp
from jax import lax
from jax.experimental import pallas as pl
from jax.experimental.pallas import tpu as pltpu

D_MODEL = 1024
D_FF = 2816
CONV_WIDTH = 512
CONV_KERNEL = 3
N_Q_HEADS = 8
N_KV_HEADS = 2
Q_PER_KV = N_Q_HEADS // N_KV_HEADS
HEAD_DIM = 64
ATTN_WIDTH = N_Q_HEADS * HEAD_DIM
KV_WIDTH = N_KV_HEADS * HEAD_DIM
WINDOW = 128
BLOCK = 128
ROPE_THETA = 500000.0
ROT_DIM = HEAD_DIM // 4
ROT_HALF = ROT_DIM // 2
MIX_WIDTH = CONV_WIDTH + ATTN_WIDTH
IN_PROJ_WIDTH = 3 * CONV_WIDTH + ATTN_WIDTH + 2 * KV_WIDTH
FFN_RES_SCALE = 0.5
RMS_EPS = 1e-5
MASK_VALUE = -1e30

LANES = 128
BF16_SUBLANES = 16
FFN_TILE = 512
MIX_TILE = 512
FFN_CHUNK = 1024
VMEM_LIMIT_BYTES = 56 * 1024 * 1024

_F32 = jnp.float32
_BF16 = jnp.bfloat16


def _rms_norm(x, gain):
    inv = lax.rsqrt(jnp.mean(x * x, axis=-1, keepdims=True) + RMS_EPS)
    return (x * inv) * gain


def _swiglu_residual(x, gain, wg_ref, wu_ref, wd_ref):
    h = _rms_norm(x, gain).astype(_BF16)
    acc = None
    for c0 in range(0, D_FF, FFN_CHUNK):
        c1 = min(c0 + FFN_CHUNK, D_FF)
        g = jnp.dot(h, wg_ref[:, c0:c1], preferred_element_type=_F32)
        u = jnp.dot(h, wu_ref[:, c0:c1], preferred_element_type=_F32)
        a = (g * jax.nn.sigmoid(g) * u).astype(_BF16)
        part = jnp.dot(a, wd_ref[c0:c1, :], preferred_element_type=_F32)
        acc = part if acc is None else acc + part
    return x + FFN_RES_SCALE * acc


def _rotate_head_pair(t, cos, sin_lo, sin_hi):
    return t * cos + pltpu.roll(t, LANES - ROT_HALF, 1) * sin_lo + pltpu.roll(t, ROT_HALF, 1) * sin_hi


def _rotary_coefficients(tile_cs, row_tab):
    ca, sa = tile_cs[:, :LANES], tile_cs[:, LANES:]
    tab = [row_tab[:, j * LANES:(j + 1) * LANES] for j in range(6)]
    return ca * tab[0] - sa * tab[1], sa * tab[2] + ca * tab[3], sa * tab[4] + ca * tab[5]


def _ffn_inproj_kernel(x_ref, g1_ref, wg_ref, wu_ref, wd_ref, g2_ref, win_ref, tile_cs_ref, row_tab_ref,
                       x1_ref, bg_ref, cu_ref, q_ref, kv_ref):
    x1 = _swiglu_residual(x_ref[...], g1_ref[...], wg_ref, wu_ref, wd_ref)
    x1_ref[...] = x1
    h = _rms_norm(x1, g2_ref[...]).astype(_BF16)
    z = jnp.dot(h, win_ref[...], preferred_element_type=_F32)
    bg_ref[...] = z[:, :CONV_WIDTH].astype(_BF16)
    cu_ref[...] = (z[:, CONV_WIDTH:2 * CONV_WIDTH] * z[:, 2 * CONV_WIDTH:3 * CONV_WIDTH]).astype(_BF16)
    cos, sin_lo, sin_hi = _rotary_coefficients(tile_cs_ref[0], row_tab_ref[...])
    q0 = 3 * CONV_WIDTH
    for j in range(ATTN_WIDTH // LANES):
        t = z[:, q0 + j * LANES:q0 + (j + 1) * LANES]
        q_ref[:, j * LANES:(j + 1) * LANES] = (
            _rotate_head_pair(t, cos, sin_lo, sin_hi) * (HEAD_DIM ** -0.5)).astype(_BF16)
    k0 = q0 + ATTN_WIDTH
    kv_ref[:, :KV_WIDTH] = _rotate_head_pair(z[:, k0:k0 + KV_WIDTH], cos, sin_lo, sin_hi).astype(_BF16)
    kv_ref[:, KV_WIDTH:] = z[:, k0 + KV_WIDTH:].astype(_BF16)


def _ffn_final_kernel(x_ref, g_ref, wg_ref, wu_ref, wd_ref, gf_ref, out_ref):
    x = _swiglu_residual(x_ref[...], g_ref[...], wg_ref, wu_ref, wd_ref)
    out_ref[...] = _rms_norm(x, gf_ref[...])


def _mixer_kernel(sinks_ref, x1_ref, bg_ref, cu_ref, cu_halo_ref, q_ref, kv_ref, kv_halo_ref,
                  convw_ref, wout_ref, x2_ref):
    first_tile = pl.program_id(1) == 0
    rows = x1_ref.shape[0]

    v = cu_ref[...].astype(_F32)
    halo = jnp.where(first_tile, 0.0, cu_halo_ref[...].astype(_F32))
    prev1 = halo[BF16_SUBLANES - 1:BF16_SUBLANES, :]
    prev2 = halo[BF16_SUBLANES - 2:BF16_SUBLANES - 1, :]
    row = lax.broadcasted_iota(jnp.int32, (rows, 1), 0)
    v1 = jnp.where(row == 0, prev1, pltpu.roll(v, 1, 0))
    v2 = jnp.where(row == 0, prev2, jnp.where(row == 1, prev1, pltpu.roll(v, 2, 0)))
    w = convw_ref[...]
    y_conv = bg_ref[...].astype(_F32) * (w[0:1, :] * v2 + w[1:2, :] * v1 + w[2:3, :] * v)

    kv_ext = jnp.concatenate([kv_halo_ref[...], kv_ref[...]], axis=0)
    qi = lax.broadcasted_iota(jnp.int32, (BLOCK, 2 * BLOCK), 0)
    kj = lax.broadcasted_iota(jnp.int32, (BLOCK, 2 * BLOCK), 1)
    in_window = (kj > qi) & (kj <= qi + BLOCK)
    attn_blocks = []
    for blk in range(rows // BLOCK):
        r0 = blk * BLOCK
        mask = in_window if blk > 0 else in_window & ((kj >= BLOCK) | jnp.logical_not(first_tile))
        heads = []
        for hkv in range(N_KV_HEADS):
            k = kv_ext[r0:r0 + 2 * BLOCK, hkv * HEAD_DIM:(hkv + 1) * HEAD_DIM]
            vv = kv_ext[r0:r0 + 2 * BLOCK, KV_WIDTH + hkv * HEAD_DIM:KV_WIDTH + (hkv + 1) * HEAD_DIM]
            for g in range(Q_PER_KV):
                hq = hkv * Q_PER_KV + g
                q = q_ref[r0:r0 + BLOCK, hq * HEAD_DIM:(hq + 1) * HEAD_DIM]
                s = lax.dot_general(q, k, (((1,), (1,)), ((), ())), preferred_element_type=_F32)
                s = jnp.where(mask, s, MASK_VALUE)
                sink = sinks_ref[hq]
                m = jnp.maximum(jnp.max(s, axis=-1, keepdims=True), sink)
                p = jnp.exp(s - m)
                denom = jnp.sum(p, axis=-1, keepdims=True) + jnp.exp(sink - m)
                heads.append(jnp.dot(p.astype(_BF16), vv, preferred_element_type=_F32) / denom)
        attn_blocks.append(jnp.concatenate(heads, axis=1))
    y_attn = jnp.concatenate(attn_blocks, axis=0)

    y = jnp.concatenate([y_conv, y_attn], axis=1).astype(_BF16)
    x2_ref[...] = x1_ref[...] + jnp.dot(y, wout_ref[...], preferred_element_type=_F32)


def _resident(shape):
    return pl.BlockSpec(shape, lambda *_: (0,) * len(shape), pipeline_mode=pl.Buffered(1))


def _rotary_tables(seq, tile):
    inv_freq = ROPE_THETA ** (-jnp.arange(0, ROT_DIM, 2, dtype=_F32) / ROT_DIM)
    dim = jnp.arange(LANES) % HEAD_DIM
    freq = jnp.where(dim < ROT_DIM, inv_freq[dim % ROT_HALF], 0.0)
    lo = (dim < ROT_HALF).astype(_F32)
    hi = ((dim >= ROT_HALF) & (dim < ROT_DIM)).astype(_F32)
    start = (jnp.arange(seq // tile) * tile).astype(_F32)[:, None] * freq[None, :]
    row = jnp.arange(tile, dtype=_F32)[:, None] * freq[None, :]
    tile_cs = jnp.concatenate([jnp.cos(start), jnp.sin(start)], axis=1).reshape(seq // tile, 1, 2 * LANES)
    cb, sb = jnp.cos(row), jnp.sin(row)
    return tile_cs, jnp.concatenate([cb, sb, -cb * lo, -sb * lo, cb * hi, sb * hi], axis=1)


def _row_spec(tile, width):
    return pl.BlockSpec((tile, width), lambda i: (i, 0))


_FFN_WEIGHT_SPECS = [_resident((D_MODEL, D_FF)), _resident((D_MODEL, D_FF)), _resident((D_FF, D_MODEL))]
_PARAMS = functools.partial(pltpu.CompilerParams, vmem_limit_bytes=VMEM_LIMIT_BYTES)


def _ffn_inproj_call(xt, gain1, wg, wu, wd, gain2, win, seq):
    tokens = xt.shape[0]
    seq_tiles = seq // FFN_TILE
    return pl.pallas_call(
        _ffn_inproj_kernel,
        grid=(tokens // FFN_TILE,),
        in_specs=[_row_spec(FFN_TILE, D_MODEL), _resident((1, D_MODEL)), *_FFN_WEIGHT_SPECS,
                  _resident((1, D_MODEL)), _resident((D_MODEL, IN_PROJ_WIDTH)),
                  pl.BlockSpec((1, 1, 2 * LANES), lambda i: (i % seq_tiles, 0, 0)), _resident((FFN_TILE, 6 * LANES))],
        out_specs=[_row_spec(FFN_TILE, D_MODEL), _row_spec(FFN_TILE, CONV_WIDTH), _row_spec(FFN_TILE, CONV_WIDTH),
                   _row_spec(FFN_TILE, ATTN_WIDTH), _row_spec(FFN_TILE, 2 * KV_WIDTH)],
        out_shape=[jax.ShapeDtypeStruct((tokens, D_MODEL), _F32),
                   jax.ShapeDtypeStruct((tokens, CONV_WIDTH), _BF16),
                   jax.ShapeDtypeStruct((tokens, CONV_WIDTH), _BF16),
                   jax.ShapeDtypeStruct((tokens, ATTN_WIDTH), _BF16),
                   jax.ShapeDtypeStruct((tokens, 2 * KV_WIDTH), _BF16)],
        compiler_params=_PARAMS(dimension_semantics=("parallel",)),
        name="ffn1_inproj",
    )(xt, gain1, wg, wu, wd, gain2, win, *_rotary_tables(seq, FFN_TILE))


def _mixer_call(sinks, x1, bg, cu, q, kv, conv_w, wout, batch):
    tokens = x1.shape[0]
    mix_tiles = tokens // batch // MIX_TILE
    halo_per_tile = MIX_TILE // BF16_SUBLANES
    blocks_per_tile = MIX_TILE // BLOCK

    def tile_spec(width):
        return pl.BlockSpec((MIX_TILE, width), lambda b, i, sinks: (b * mix_tiles + i, 0))

    return pl.pallas_call(
        _mixer_kernel,
        grid_spec=pltpu.PrefetchScalarGridSpec(
            num_scalar_prefetch=1,
            grid=(batch, mix_tiles),
            in_specs=[
                tile_spec(D_MODEL), tile_spec(CONV_WIDTH), tile_spec(CONV_WIDTH),
                pl.BlockSpec((BF16_SUBLANES, CONV_WIDTH),
                             lambda b, i, sinks: (jnp.maximum((b * mix_tiles + i) * halo_per_tile - 1, 0), 0)),
                tile_spec(ATTN_WIDTH), tile_spec(2 * KV_WIDTH),
                pl.BlockSpec((BLOCK, 2 * KV_WIDTH),
                             lambda b, i, sinks: (jnp.maximum((b * mix_tiles + i) * blocks_per_tile - 1, 0), 0)),
                pl.BlockSpec((CONV_KERNEL, CONV_WIDTH), lambda b, i, sinks: (0, 0), pipeline_mode=pl.Buffered(1)),
                pl.BlockSpec((MIX_WIDTH, D_MODEL), lambda b, i, sinks: (0, 0), pipeline_mode=pl.Buffered(1)),
            ],
            out_specs=tile_spec(D_MODEL),
        ),
        out_shape=jax.ShapeDtypeStruct((tokens, D_MODEL), _F32),
        compiler_params=_PARAMS(dimension_semantics=("parallel", "parallel")),
        name="mixer_outproj",
    )(sinks, x1, bg, cu, cu, q, kv, kv, conv_w, wout)


def _ffn_final_call(x2, gain, wg, wu, wd, final_gain):
    tokens = x2.shape[0]
    return pl.pallas_call(
        _ffn_final_kernel,
        grid=(tokens // FFN_TILE,),
        in_specs=[_row_spec(FFN_TILE, D_MODEL), _resident((1, D_MODEL)), *_FFN_WEIGHT_SPECS,
                  _resident((1, D_MODEL))],
        out_specs=_row_spec(FFN_TILE, D_MODEL),
        out_shape=jax.ShapeDtypeStruct((tokens, D_MODEL), _F32),
        compiler_params=_PARAMS(dimension_semantics=("parallel",)),
        name="ffn2_final",
    )(x2, gain, wg, wu, wd, final_gain)


def kernel(x, ffn1_norm, ffn1_w_gate, ffn1_w_up, ffn1_w_down, mix_norm, w_in, conv_w, attn_sinks, w_out,
           ffn2_norm, ffn2_w_gate, ffn2_w_up, ffn2_w_down, final_norm):
    batch, seq, d_model = x.shape
    assert d_model == D_MODEL and ffn1_norm.shape[0] == 1
    assert seq % MIX_TILE == 0 and seq % FFN_TILE == 0
    xt = x.reshape(batch * seq, D_MODEL)
    x1, bg, cu, q, kv = _ffn_inproj_call(
        xt, ffn1_norm, ffn1_w_gate[0].astype(_BF16), ffn1_w_up[0].astype(_BF16), ffn1_w_down[0].astype(_BF16),
        mix_norm, w_in[0].astype(_BF16), seq)
    x2 = _mixer_call(attn_sinks[0], x1, bg, cu, q, kv, conv_w[0], w_out[0].astype(_BF16), batch)
    out = _ffn_final_call(
        x2, ffn2_norm, ffn2_w_gate[0].astype(_BF16), ffn2_w_up[0].astype(_BF16), ffn2_w_down[0].astype(_BF16),
        final_norm.reshape(1, D_MODEL))
    return out.reshape(batch, seq, D_MODEL)
```

```python
import functools

import jax
import jax.numpy as jnp
from jax import lax
from jax.experimental import pallas as pl
from jax.experimental.pallas import tpu as pltpu

D_MODEL = 1024
D_FF = 2816
CONV_WIDTH = 512
CONV_KERNEL = 3
N_Q_HEADS = 8
N_KV_HEADS = 2
Q_PER_KV = N_Q_HEADS // N_KV_HEADS
HEAD_DIM = 64
ATTN_WIDTH = N_Q_HEADS * HEAD_DIM
KV_WIDTH = N_KV_HEADS * HEAD_DIM
WINDOW = 128
BLOCK = 128
ROPE_THETA = 500000.0
ROT_DIM = HEAD_DIM // 4
ROT_HALF = ROT_DIM // 2
MIX_WIDTH = CONV_WIDTH + ATTN_WIDTH
IN_PROJ_WIDTH = 3 * CONV_WIDTH + ATTN_WIDTH + 2 * KV_WIDTH
FFN_RES_SCALE = 0.5
RMS_EPS = 1e-5
MASK_VALUE = -1e30

LANES = 128
BF16_SUBLANES = 16
FFN_TILE = 512
MIX_TILE = 512
FFN_CHUNK = 1024
VMEM_LIMIT_BYTES = 56 * 1024 * 1024

_F32 = jnp.float32
_BF16 = jnp.bfloat16


def _rms_norm(x, gain):
    inv = lax.rsqrt(jnp.mean(x * x, axis=-1, keepdims=True) + RMS_EPS)
    return (x * inv) * gain


def _ffn_dots(h, wg_ref, wu_ref, wd_ref, result):
    acc = None
    for c0 in range(0, D_FF, FFN_CHUNK):
        c1 = min(c0 + FFN_CHUNK, D_FF)
        g = jnp.dot(h, wg_ref[:, c0:c1], preferred_element_type=_F32)
        yield
        u = jnp.dot(h, wu_ref[:, c0:c1], preferred_element_type=_F32)
        yield
        a = (g * jax.nn.sigmoid(g) * u).astype(_BF16)
        part = jnp.dot(a, wd_ref[c0:c1, :], preferred_element_type=_F32)
        acc = part if acc is None else acc + part
        result["acc"] = acc
        yield


def _swiglu_residual(x, gain, wg_ref, wu_ref, wd_ref):
    result = {}
    for _ in _ffn_dots(_rms_norm(x, gain).astype(_BF16), wg_ref, wu_ref, wd_ref, result):
        pass
    return x + FFN_RES_SCALE * result["acc"]


def _rotate_head_pair(t, cos, sin_lo, sin_hi):
    return t * cos + pltpu.roll(t, LANES - ROT_HALF, 1) * sin_lo + pltpu.roll(t, ROT_HALF, 1) * sin_hi


def _rotary_coefficients(tile_cs, row_tab):
    ca, sa = tile_cs[:, :LANES], tile_cs[:, LANES:]
    tab = [row_tab[:, j * LANES:(j + 1) * LANES] for j in range(6)]
    return ca * tab[0] - sa * tab[1], sa * tab[2] + ca * tab[3], sa * tab[4] + ca * tab[5]


def _ffn_inproj_kernel(x_ref, g1_ref, wg_ref, wu_ref, wd_ref, g2_ref, win_ref, tile_cs_ref, row_tab_ref,
                       x1_ref, bg_ref, cu_ref, q_ref, kv_ref):
    x1 = _swiglu_residual(x_ref[...], g1_ref[...], wg_ref, wu_ref, wd_ref)
    x1_ref[...] = x1
    h = _rms_norm(x1, g2_ref[...]).astype(_BF16)
    z = jnp.dot(h, win_ref[...], preferred_element_type=_F32)
    bg_ref[...] = z[:, :CONV_WIDTH].astype(_BF16)
    cu_ref[...] = (z[:, CONV_WIDTH:2 * CONV_WIDTH] * z[:, 2 * CONV_WIDTH:3 * CONV_WIDTH]).astype(_BF16)
    cos, sin_lo, sin_hi = _rotary_coefficients(tile_cs_ref[0], row_tab_ref[...])
    q0 = 3 * CONV_WIDTH
    for j in range(ATTN_WIDTH // LANES):
        t = z[:, q0 + j * LANES:q0 + (j + 1) * LANES]
        q_ref[:, j * LANES:(j + 1) * LANES] = (
            _rotate_head_pair(t, cos, sin_lo, sin_hi) * (HEAD_DIM ** -0.5)).astype(_BF16)
    k0 = q0 + ATTN_WIDTH
    kv_ref[:, :KV_WIDTH] = _rotate_head_pair(z[:, k0:k0 + KV_WIDTH], cos, sin_lo, sin_hi).astype(_BF16)
    kv_ref[:, KV_WIDTH:] = z[:, k0 + KV_WIDTH:].astype(_BF16)


def _conv_mixer(first_tile, bg_ref, cu_ref, cu_halo_ref, convw_ref):
    rows = cu_ref.shape[0]
    v = cu_ref[...].astype(_F32)
    halo = jnp.where(first_tile, 0.0, cu_halo_ref[...].astype(_F32))
    prev1 = halo[BF16_SUBLANES - 1:BF16_SUBLANES, :]
    prev2 = halo[BF16_SUBLANES - 2:BF16_SUBLANES - 1, :]
    row = lax.broadcasted_iota(jnp.int32, (rows, 1), 0)
    v1 = jnp.where(row == 0, prev1, pltpu.roll(v, 1, 0))
    v2 = jnp.where(row == 0, prev2, jnp.where(row == 1, prev1, pltpu.roll(v, 2, 0)))
    w = convw_ref[...]
    return bg_ref[...].astype(_F32) * (w[0:1, :] * v2 + w[1:2, :] * v1 + w[2:3, :] * v)


def _attention_pieces(first_tile, sinks_ref, q_ref, kv_ref, kv_halo_ref, result):
    rows = q_ref.shape[0]
    kv_ext = jnp.concatenate([kv_halo_ref[...], kv_ref[...]], axis=0)
    qi = lax.broadcasted_iota(jnp.int32, (BLOCK, 2 * BLOCK), 0)
    kj = lax.broadcasted_iota(jnp.int32, (BLOCK, 2 * BLOCK), 1)
    in_window = (kj > qi) & (kj <= qi + BLOCK)
    first_mask = in_window & ((kj >= BLOCK) | jnp.logical_not(first_tile))

    def scores(blk, hkv):
        r0 = blk * BLOCK
        k = kv_ext[r0:r0 + 2 * BLOCK, hkv * HEAD_DIM:(hkv + 1) * HEAD_DIM]
        out = []
        for g in range(Q_PER_KV):
            hq = hkv * Q_PER_KV + g
            q = q_ref[r0:r0 + BLOCK, hq * HEAD_DIM:(hq + 1) * HEAD_DIM]
            out.append(lax.dot_general(q, k, (((1,), (1,)), ((), ())), preferred_element_type=_F32))
        return out

    def weighted_values(blk, hkv, score_list):
        r0 = blk * BLOCK
        vv = kv_ext[r0:r0 + 2 * BLOCK, KV_WIDTH + hkv * HEAD_DIM:KV_WIDTH + (hkv + 1) * HEAD_DIM]
        mask = in_window if blk > 0 else first_mask
        out = []
        for g, s in enumerate(score_list):
            sink = sinks_ref[hkv * Q_PER_KV + g]
            s = jnp.where(mask, s, MASK_VALUE)
            m = jnp.maximum(jnp.max(s, axis=-1, keepdims=True), sink)
            p = jnp.exp(s - m)
            denom = jnp.sum(p, axis=-1, keepdims=True) + jnp.exp(sink - m)
            out.append(jnp.dot(p.astype(_BF16), vv, preferred_element_type=_F32) / denom)
        return out

    pieces = [(blk, hkv) for blk in range(rows // BLOCK) for hkv in range(N_KV_HEADS)]
    pending = scores(*pieces[0])
    yield
    blocks, heads = [], []
    for i, piece in enumerate(pieces):
        heads.extend(weighted_values(*piece, pending))
        if len(heads) == N_Q_HEADS:
            blocks.append(jnp.concatenate(heads, axis=1))
            heads = []
        if i + 1 < len(pieces):
            pending = scores(*pieces[i + 1])
            yield
    result["y_attn"] = jnp.concatenate(blocks, axis=0)


def _mixer_ffn_kernel(sinks_ref, x1_ref, bg_ref, cu_ref, cu_halo_ref, q_ref, kv_ref, kv_halo_ref, convw_ref, wout_ref,
                      g_ref, wg_ref, wu_ref, wd_ref, gf_ref, out_ref, x2_ref, h_ref, *, tiles_per_seq):
    step = pl.program_id(0)

    @pl.when(step == 0)
    def _():
        x2_ref[1] = jnp.zeros(x2_ref.shape[1:], x2_ref.dtype)
        h_ref[1] = jnp.zeros(h_ref.shape[1:], h_ref.dtype)

    read_slot = (step + 1) % 2
    write_slot = step % 2
    first_tile = step % tiles_per_seq == 0
    ffn, att = {}, {}
    ffn_dots = _ffn_dots(h_ref[read_slot], wg_ref, wu_ref, wd_ref, ffn)
    attention = _attention_pieces(first_tile, sinks_ref, q_ref, kv_ref, kv_halo_ref, att)
    next(ffn_dots)
    y_conv = _conv_mixer(first_tile, bg_ref, cu_ref, cu_halo_ref, convw_ref)
    for _ in attention:
        next(ffn_dots, None)
    y = jnp.concatenate([y_conv, att["y_attn"]], axis=1).astype(_BF16)
    x2 = x1_ref[...] + jnp.dot(y, wout_ref[...], preferred_element_type=_F32)
    x2_ref[write_slot] = x2
    h_ref[write_slot] = _rms_norm(x2, g_ref[...]).astype(_BF16)
    for _ in ffn_dots:
        pass
    x = x2_ref[read_slot] + FFN_RES_SCALE * ffn["acc"]
    out_ref[...] = _rms_norm(x, gf_ref[...])


def _resident(shape):
    return pl.BlockSpec(shape, lambda *_: (0,) * len(shape), pipeline_mode=pl.Buffered(1))


def _rotary_tables(seq, tile):
    inv_freq = ROPE_THETA ** (-jnp.arange(0, ROT_DIM, 2, dtype=_F32) / ROT_DIM)
    dim = jnp.arange(LANES) % HEAD_DIM
    freq = jnp.where(dim < ROT_DIM, inv_freq[dim % ROT_HALF], 0.0)
    lo = (dim < ROT_HALF).astype(_F32)
    hi = ((dim >= ROT_HALF) & (dim < ROT_DIM)).astype(_F32)
    start = (jnp.arange(seq // tile) * tile).astype(_F32)[:, None] * freq[None, :]
    row = jnp.arange(tile, dtype=_F32)[:, None] * freq[None, :]
    tile_cs = jnp.concatenate([jnp.cos(start), jnp.sin(start)], axis=1).reshape(seq // tile, 1, 2 * LANES)
    cb, sb = jnp.cos(row), jnp.sin(row)
    return tile_cs, jnp.concatenate([cb, sb, -cb * lo, -sb * lo, cb * hi, sb * hi], axis=1)


def _row_spec(tile, width):
    return pl.BlockSpec((tile, width), lambda i: (i, 0))


_FFN_WEIGHT_SPECS = [_resident((D_MODEL, D_FF)), _resident((D_MODEL, D_FF)), _resident((D_FF, D_MODEL))]
_PARAMS = functools.partial(pltpu.CompilerParams, vmem_limit_bytes=VMEM_LIMIT_BYTES)


def _ffn_inproj_call(xt, gain1, wg, wu, wd, gain2, win, seq):
    tokens = xt.shape[0]
    seq_tiles = seq // FFN_TILE
    return pl.pallas_call(
        _ffn_inproj_kernel,
        grid=(tokens // FFN_TILE,),
        in_specs=[_row_spec(FFN_TILE, D_MODEL), _resident((1, D_MODEL)), *_FFN_WEIGHT_SPECS,
                  _resident((1, D_MODEL)), _resident((D_MODEL, IN_PROJ_WIDTH)),
                  pl.BlockSpec((1, 1, 2 * LANES), lambda i: (i % seq_tiles, 0, 0)), _resident((FFN_TILE, 6 * LANES))],
        out_specs=[_row_spec(FFN_TILE, D_MODEL), _row_spec(FFN_TILE, CONV_WIDTH), _row_spec(FFN_TILE, CONV_WIDTH),
                   _row_spec(FFN_TILE, ATTN_WIDTH), _row_spec(FFN_TILE, 2 * KV_WIDTH)],
        out_shape=[jax.ShapeDtypeStruct((tokens, D_MODEL), _F32),
                   jax.ShapeDtypeStruct((tokens, CONV_WIDTH), _BF16),
                   jax.ShapeDtypeStruct((tokens, CONV_WIDTH), _BF16),
                   jax.ShapeDtypeStruct((tokens, ATTN_WIDTH), _BF16),
                   jax.ShapeDtypeStruct((tokens, 2 * KV_WIDTH), _BF16)],
        compiler_params=_PARAMS(dimension_semantics=("parallel",)),
        name="ffn1_inproj",
    )(xt, gain1, wg, wu, wd, gain2, win, *_rotary_tables(seq, FFN_TILE))


def _mixer_ffn_call(sinks, x1, bg, cu, q, kv, conv_w, wout, gain, wg, wu, wd, final_gain, seq):
    tokens = x1.shape[0]
    n_tiles = tokens // MIX_TILE
    halo_per_tile = MIX_TILE // BF16_SUBLANES
    blocks_per_tile = MIX_TILE // BLOCK

    def mix_tile(s):
        return jnp.minimum(s, n_tiles - 1)

    def tile_spec(width):
        return pl.BlockSpec((MIX_TILE, width), lambda s, sinks: (mix_tile(s), 0))

    return pl.pallas_call(
        functools.partial(_mixer_ffn_kernel, tiles_per_seq=seq // MIX_TILE),
        grid_spec=pltpu.PrefetchScalarGridSpec(
            num_scalar_prefetch=1,
            grid=(n_tiles + 1,),
            in_specs=[
                tile_spec(D_MODEL), tile_spec(CONV_WIDTH), tile_spec(CONV_WIDTH),
                pl.BlockSpec((BF16_SUBLANES, CONV_WIDTH),
                             lambda s, sinks: (jnp.maximum(mix_tile(s) * halo_per_tile - 1, 0), 0)),
                tile_spec(ATTN_WIDTH), tile_spec(2 * KV_WIDTH),
                pl.BlockSpec((BLOCK, 2 * KV_WIDTH),
                             lambda s, sinks: (jnp.maximum(mix_tile(s) * blocks_per_tile - 1, 0), 0)),
                _resident((CONV_KERNEL, CONV_WIDTH)), _resident((MIX_WIDTH, D_MODEL)),
                _resident((1, D_MODEL)), *_FFN_WEIGHT_SPECS, _resident((1, D_MODEL)),
            ],
            out_specs=pl.BlockSpec((MIX_TILE, D_MODEL), lambda s, sinks: (jnp.maximum(s - 1, 0), 0)),
            scratch_shapes=[pltpu.VMEM((2, MIX_TILE, D_MODEL), _F32), pltpu.VMEM((2, MIX_TILE, D_MODEL), _BF16)],
        ),
        out_shape=jax.ShapeDtypeStruct((tokens, D_MODEL), _F32),
        compiler_params=_PARAMS(dimension_semantics=("arbitrary",)),
        name="mixer_ffn2",
    )(sinks, x1, bg, cu, cu, q, kv, kv, conv_w, wout, gain, wg, wu, wd, final_gain)


def kernel(x, ffn1_norm, ffn1_w_gate, ffn1_w_up, ffn1_w_down, mix_norm, w_in, conv_w, attn_sinks, w_out,
           ffn2_norm, ffn2_w_gate, ffn2_w_up, ffn2_w_down, final_norm):
    batch, seq, d_model = x.shape
    assert d_model == D_MODEL and ffn1_norm.shape[0] == 1
    assert seq % MIX_TILE == 0 and seq % FFN_TILE == 0
    xt = x.reshape(batch * seq, D_MODEL)
    x1, bg, cu, q, kv = _ffn_inproj_call(
        xt, ffn1_norm, ffn1_w_gate[0].astype(_BF16), ffn1_w_up[0].astype(_BF16), ffn1_w_down[0].astype(_BF16),
        mix_norm, w_in[0].astype(_BF16), seq)
    out = _mixer_ffn_call(
        attn_sinks[0], x1, bg, cu, q, kv, conv_w[0], w_out[0].astype(_BF16),
        ffn2_norm, ffn2_w_gate[0].astype(_BF16), ffn2_w_up[0].astype(_BF16), ffn2_w_down[0].astype(_BF16),
        final_norm.reshape(1, D_MODEL), seq)
    return out.reshape(batch, seq, D_MODEL)
```

```python
import functools

import jax
import jax.numpy as jnp
from jax import lax
from jax.experimental import pallas as pl
from jax.experimental.pallas import tpu as pltpu

D_MODEL = 1024
D_FF = 2816
CONV_WIDTH = 512
CONV_KERNEL = 3
N_Q_HEADS = 8
N_KV_HEADS = 2
Q_PER_KV = N_Q_HEADS // N_KV_HEADS
HEAD_DIM = 64
ATTN_WIDTH = N_Q_HEADS * HEAD_DIM
KV_WIDTH = N_KV_HEADS * HEAD_DIM
WINDOW = 128
BLOCK = 128
ROPE_THETA = 500000.0
ROT_DIM = HEAD_DIM // 4
ROT_HALF = ROT_DIM // 2
MIX_WIDTH = CONV_WIDTH + ATTN_WIDTH
IN_PROJ_WIDTH = 3 * CONV_WIDTH + ATTN_WIDTH + 2 * KV_WIDTH
FFN_RES_SCALE = 0.5
RMS_EPS = 1e-5
MASK_VALUE = -1e30

LANES = 128
BF16_SUBLANES = 16
FFN_TILE = 512
MIX_TILE = 512
FFN1_CHUNK = 1024
FFN2_CHUNK = 512
SCORE_LOOKAHEAD = 2
VMEM_LIMIT_BYTES = 56 * 1024 * 1024

_F32 = jnp.float32
_BF16 = jnp.bfloat16


def _rms_norm(x, gain):
    inv = lax.rsqrt(jnp.mean(x * x, axis=-1, keepdims=True) + RMS_EPS)
    return (x * inv) * gain


def _ffn_dots(h, wg_ref, wu_ref, wd_ref, chunk, result):
    acc = None
    for c0 in range(0, D_FF, chunk):
        c1 = min(c0 + chunk, D_FF)
        g = jnp.dot(h, wg_ref[:, c0:c1], preferred_element_type=_F32)
        yield
        u = jnp.dot(h, wu_ref[:, c0:c1], preferred_element_type=_F32)
        yield
        a = (g * jax.nn.sigmoid(g) * u).astype(_BF16)
        part = jnp.dot(a, wd_ref[c0:c1, :], preferred_element_type=_F32)
        acc = part if acc is None else acc + part
        result["acc"] = acc
        yield


def _swiglu_residual(x, gain, wg_ref, wu_ref, wd_ref):
    result = {}
    for _ in _ffn_dots(_rms_norm(x, gain).astype(_BF16), wg_ref, wu_ref, wd_ref, FFN1_CHUNK, result):
        pass
    return x + FFN_RES_SCALE * result["acc"]


def _rotate_head_pair(t, cos, sin_lo, sin_hi):
    return t * cos + pltpu.roll(t, LANES - ROT_HALF, 1) * sin_lo + pltpu.roll(t, ROT_HALF, 1) * sin_hi


def _rotary_coefficients(tile_cs, row_tab):
    ca, sa = tile_cs[:, :LANES], tile_cs[:, LANES:]
    tab = [row_tab[:, j * LANES:(j + 1) * LANES] for j in range(6)]
    return ca * tab[0] - sa * tab[1], sa * tab[2] + ca * tab[3], sa * tab[4] + ca * tab[5]


def _ffn_inproj_kernel(x_ref, g1_ref, wg_ref, wu_ref, wd_ref, g2_ref, win_ref, tile_cs_ref, row_tab_ref,
                       x1_ref, bg_ref, cu_ref, q_ref, kv_ref):
    x1 = _swiglu_residual(x_ref[...], g1_ref[...], wg_ref, wu_ref, wd_ref)
    x1_ref[...] = x1
    h = _rms_norm(x1, g2_ref[...]).astype(_BF16)
    z = jnp.dot(h, win_ref[...], preferred_element_type=_F32)
    bg_ref[...] = z[:, :CONV_WIDTH].astype(_BF16)
    cu_ref[...] = (z[:, CONV_WIDTH:2 * CONV_WIDTH] * z[:, 2 * CONV_WIDTH:3 * CONV_WIDTH]).astype(_BF16)
    cos, sin_lo, sin_hi = _rotary_coefficients(tile_cs_ref[0], row_tab_ref[...])
    q0 = 3 * CONV_WIDTH
    for j in range(ATTN_WIDTH // LANES):
        t = z[:, q0 + j * LANES:q0 + (j + 1) * LANES]
        q_ref[:, j * LANES:(j + 1) * LANES] = (
            _rotate_head_pair(t, cos, sin_lo, sin_hi) * (HEAD_DIM ** -0.5)).astype(_BF16)
    k0 = q0 + ATTN_WIDTH
    kv_ref[:, :KV_WIDTH] = _rotate_head_pair(z[:, k0:k0 + KV_WIDTH], cos, sin_lo, sin_hi).astype(_BF16)
    kv_ref[:, KV_WIDTH:] = z[:, k0 + KV_WIDTH:].astype(_BF16)


def _conv_mixer(first_tile, bg_ref, cu_ref, cu_halo_ref, convw_ref):
    rows = cu_ref.shape[0]
    v = cu_ref[...].astype(_F32)
    halo = jnp.where(first_tile, 0.0, cu_halo_ref[...].astype(_F32))
    prev1 = halo[BF16_SUBLANES - 1:BF16_SUBLANES, :]
    prev2 = halo[BF16_SUBLANES - 2:BF16_SUBLANES - 1, :]
    row = lax.broadcasted_iota(jnp.int32, (rows, 1), 0)
    v1 = jnp.where(row == 0, prev1, pltpu.roll(v, 1, 0))
    v2 = jnp.where(row == 0, prev2, jnp.where(row == 1, prev1, pltpu.roll(v, 2, 0)))
    w = convw_ref[...]
    return bg_ref[...].astype(_F32) * (w[0:1, :] * v2 + w[1:2, :] * v1 + w[2:3, :] * v)


def _attention_pieces(first_tile, sinks_ref, q_ref, kv_ref, kv_halo_ref, result):
    rows = q_ref.shape[0]
    kv_ext = jnp.concatenate([kv_halo_ref[...], kv_ref[...]], axis=0)
    qi = lax.broadcasted_iota(jnp.int32, (BLOCK, 2 * BLOCK), 0)
    kj = lax.broadcasted_iota(jnp.int32, (BLOCK, 2 * BLOCK), 1)
    in_window = (kj > qi) & (kj <= qi + BLOCK)
    first_mask = in_window & ((kj >= BLOCK) | jnp.logical_not(first_tile))

    def scores(blk, hkv):
        r0 = blk * BLOCK
        k = kv_ext[r0:r0 + 2 * BLOCK, hkv * HEAD_DIM:(hkv + 1) * HEAD_DIM]
        out = []
        for g in range(Q_PER_KV):
            hq = hkv * Q_PER_KV + g
            q = q_ref[r0:r0 + BLOCK, hq * HEAD_DIM:(hq + 1) * HEAD_DIM]
            out.append(lax.dot_general(q, k, (((1,), (1,)), ((), ())), preferred_element_type=_F32))
        return out

    def weighted_values(blk, hkv, score_list):
        r0 = blk * BLOCK
        vv = kv_ext[r0:r0 + 2 * BLOCK, KV_WIDTH + hkv * HEAD_DIM:KV_WIDTH + (hkv + 1) * HEAD_DIM]
        mask = in_window if blk > 0 else first_mask
        out = []
        for g, s in enumerate(score_list):
            sink = sinks_ref[hkv * Q_PER_KV + g]
            s = jnp.where(mask, s, MASK_VALUE)
            m = jnp.maximum(jnp.max(s, axis=-1, keepdims=True), sink)
            p = jnp.exp(s - m)
            denom = jnp.sum(p, axis=-1, keepdims=True) + jnp.exp(sink - m)
            out.append(jnp.dot(p.astype(_BF16), vv, preferred_element_type=_F32) / denom)
        return out

    pieces = [(blk, hkv) for blk in range(rows // BLOCK) for hkv in range(N_KV_HEADS)]
    pending = [scores(*piece) for piece in pieces[:SCORE_LOOKAHEAD]]
    yield
    blocks, heads = [], []
    for i, piece in enumerate(pieces):
        heads.extend(weighted_values(*piece, pending.pop(0)))
        if len(heads) == N_Q_HEADS:
            blocks.append(jnp.concatenate(heads, axis=1))
            heads = []
        if i + SCORE_LOOKAHEAD < len(pieces):
            pending.append(scores(*pieces[i + SCORE_LOOKAHEAD]))
        if i + 1 < len(pieces):
            yield
    result["y_attn"] = jnp.concatenate(blocks, axis=0)


def _mixer_ffn_kernel(sinks_ref, x1_ref, bg_ref, cu_ref, cu_halo_ref, q_ref, kv_ref, kv_halo_ref, convw_ref, wout_ref,
                      g_ref, wg_ref, wu_ref, wd_ref, gf_ref, out_ref, x2_ref, h_ref, *, tiles_per_seq):
    step = pl.program_id(0)

    @pl.when(step == 0)
    def _():
        x2_ref[1] = jnp.zeros(x2_ref.shape[1:], x2_ref.dtype)
        h_ref[1] = jnp.zeros(h_ref.shape[1:], h_ref.dtype)

    read_slot = (step + 1) % 2
    write_slot = step % 2
    first_tile = step % tiles_per_seq == 0
    ffn, att = {}, {}
    ffn_dots = _ffn_dots(h_ref[read_slot], wg_ref, wu_ref, wd_ref, FFN2_CHUNK, ffn)
    attention = _attention_pieces(first_tile, sinks_ref, q_ref, kv_ref, kv_halo_ref, att)
    next(ffn_dots)
    y_conv = _conv_mixer(first_tile, bg_ref, cu_ref, cu_halo_ref, convw_ref)
    for _ in attention:
        next(ffn_dots, None)
    y = jnp.concatenate([y_conv, att["y_attn"]], axis=1).astype(_BF16)
    x2 = x1_ref[...] + jnp.dot(y, wout_ref[...], preferred_element_type=_F32)
    x2_ref[write_slot] = x2
    h_ref[write_slot] = _rms_norm(x2, g_ref[...]).astype(_BF16)
    for _ in ffn_dots:
        pass
    x = x2_ref[read_slot] + FFN_RES_SCALE * ffn["acc"]
    out_ref[...] = _rms_norm(x, gf_ref[...])


def _resident(shape):
    return pl.BlockSpec(shape, lambda *_: (0,) * len(shape), pipeline_mode=pl.Buffered(1))


def _rotary_tables(seq, tile):
    inv_freq = ROPE_THETA ** (-jnp.arange(0, ROT_DIM, 2, dtype=_F32) / ROT_DIM)
    dim = jnp.arange(LANES) % HEAD_DIM
    freq = jnp.where(dim < ROT_DIM, inv_freq[dim % ROT_HALF], 0.0)
    lo = (dim < ROT_HALF).astype(_F32)
    hi = ((dim >= ROT_HALF) & (dim < ROT_DIM)).astype(_F32)
    start = (jnp.arange(seq // tile) * tile).astype(_F32)[:, None] * freq[None, :]
    row = jnp.arange(tile, dtype=_F32)[:, None] * freq[None, :]
    tile_cs = jnp.concatenate([jnp.cos(start), jnp.sin(start)], axis=1).reshape(seq // tile, 1, 2 * LANES)
    cb, sb = jnp.cos(row), jnp.sin(row)
    return tile_cs, jnp.concatenate([cb, sb, -cb * lo, -sb * lo, cb * hi, sb * hi], axis=1)


def _row_spec(tile, width):
    return pl.BlockSpec((tile, width), lambda i: (i, 0))


_FFN_WEIGHT_SPECS = [_resident((D_MODEL, D_FF)), _resident((D_MODEL, D_FF)), _resident((D_FF, D_MODEL))]
_PARAMS = functools.partial(pltpu.CompilerParams, vmem_limit_bytes=VMEM_LIMIT_BYTES)


def _ffn_inproj_call(xt, gain1, wg, wu, wd, gain2, win, seq):
    tokens = xt.shape[0]
    seq_tiles = seq // FFN_TILE
    return pl.pallas_call(
        _ffn_inproj_kernel,
        grid=(tokens // FFN_TILE,),
        in_specs=[_row_spec(FFN_TILE, D_MODEL), _resident((1, D_MODEL)), *_FFN_WEIGHT_SPECS,
                  _resident((1, D_MODEL)), _resident((D_MODEL, IN_PROJ_WIDTH)),
                  pl.BlockSpec((1, 1, 2 * LANES), lambda i: (i % seq_tiles, 0, 0)), _resident((FFN_TILE, 6 * LANES))],
        out_specs=[_row_spec(FFN_TILE, D_MODEL), _row_spec(FFN_TILE, CONV_WIDTH), _row_spec(FFN_TILE, CONV_WIDTH),
                   _row_spec(FFN_TILE, ATTN_WIDTH), _row_spec(FFN_TILE, 2 * KV_WIDTH)],
        out_shape=[jax.ShapeDtypeStruct((tokens, D_MODEL), _F32),
                   jax.ShapeDtypeStruct((tokens, CONV_WIDTH), _BF16),
                   jax.ShapeDtypeStruct((tokens, CONV_WIDTH), _BF16),
                   jax.ShapeDtypeStruct((tokens, ATTN_WIDTH), _BF16),
                   jax.ShapeDtypeStruct((tokens, 2 * KV_WIDTH), _BF16)],
        compiler_params=_PARAMS(dimension_semantics=("parallel",)),
        name="ffn1_inproj",
    )(xt, gain1, wg, wu, wd, gain2, win, *_rotary_tables(seq, FFN_TILE))


def _mixer_ffn_call(sinks, x1, bg, cu, q, kv, conv_w, wout, gain, wg, wu, wd, final_gain, seq):
    tokens = x1.shape[0]
    n_tiles = tokens // MIX_TILE
    halo_per_tile = MIX_TILE // BF16_SUBLANES
    blocks_per_tile = MIX_TILE // BLOCK

    def mix_tile(s):
        return jnp.minimum(s, n_tiles - 1)

    def tile_spec(width):
        return pl.BlockSpec((MIX_TILE, width), lambda s, sinks: (mix_tile(s), 0))

    return pl.pallas_call(
        functools.partial(_mixer_ffn_kernel, tiles_per_seq=seq // MIX_TILE),
        grid_spec=pltpu.PrefetchScalarGridSpec(
            num_scalar_prefetch=1,
            grid=(n_tiles + 1,),
            in_specs=[
                tile_spec(D_MODEL), tile_spec(CONV_WIDTH), tile_spec(CONV_WIDTH),
                pl.BlockSpec((BF16_SUBLANES, CONV_WIDTH),
                             lambda s, sinks: (jnp.maximum(mix_tile(s) * halo_per_tile - 1, 0), 0)),
                tile_spec(ATTN_WIDTH), tile_spec(2 * KV_WIDTH),
                pl.BlockSpec((BLOCK, 2 * KV_WIDTH),
                             lambda s, sinks: (jnp.maximum(mix_tile(s) * blocks_per_tile - 1, 0), 0)),
                _resident((CONV_KERNEL, CONV_WIDTH)), _resident((MIX_WIDTH, D_MODEL)),
                _resident((1, D_MODEL)), *_FFN_WEIGHT_SPECS, _resident((1, D_MODEL)),
            ],
            out_specs=pl.BlockSpec((MIX_TILE, D_MODEL), lambda s, sinks: (jnp.maximum(s - 1, 0), 0)),
            scratch_shapes=[pltpu.VMEM((2, MIX_TILE, D_MODEL), _F32), pltpu.VMEM((2, MIX_TILE, D_MODEL), _BF16)],
        ),
        out_shape=jax.ShapeDtypeStruct((tokens, D_MODEL), _F32),
        compiler_params=_PARAMS(dimension_semantics=("arbitrary",)),
        name="mixer_ffn2",
    )(sinks, x1, bg, cu, cu, q, kv, kv, conv_w, wout, gain, wg, wu, wd, final_gain)


def kernel(x, ffn1_norm, ffn1_w_gate, ffn1_w_up, ffn1_w_down, mix_norm, w_in, conv_w, attn_sinks, w_out,
           ffn2_norm, ffn2_w_gate, ffn2_w_up, ffn2_w_down, final_norm):
    batch, seq, d_model = x.shape
    assert d_model == D_MODEL and ffn1_norm.shape[0] == 1
    assert seq % MIX_TILE == 0 and seq % FFN_TILE == 0
    xt = x.reshape(batch * seq, D_MODEL)
    x1, bg, cu, q, kv = _ffn_inproj_call(
        xt, ffn1_norm, ffn1_w_gate[0].astype(_BF16), ffn1_w_up[0].astype(_BF16), ffn1_w_down[0].astype(_BF16),
        mix_norm, w_in[0].astype(_BF16), seq)
    out = _mixer_ffn_call(
        attn_sinks[0], x1, bg, cu, q, kv, conv_w[0], w_out[0].astype(_BF16),
        ffn2_norm, ffn2_w_gate[0].astype(_BF16), ffn2_w_up[0].astype(_BF16), ffn2_w_down[0].astype(_BF16),
        final_norm.reshape(1, D_MODEL), seq)
    return out.reshape(batch, seq, D_MODEL)
```

```python
import functools

import jax
import jax.numpy as jnp
from jax import lax
from jax.experimental import pallas as pl
from jax.experimental.pallas import tpu as pltpu

D_MODEL = 1024
D_FF = 2816
CONV_WIDTH = 512
CONV_KERNEL = 3
N_Q_HEADS = 8
N_KV_HEADS = 2
Q_PER_KV = N_Q_HEADS // N_KV_HEADS
HEAD_DIM = 64
ATTN_WIDTH = N_Q_HEADS * HEAD_DIM
KV_WIDTH = N_KV_HEADS * HEAD_DIM
WINDOW = 128
BLOCK = 128
ROPE_THETA = 500000.0
ROT_DIM = HEAD_DIM // 4
ROT_HALF = ROT_DIM // 2
MIX_WIDTH = CONV_WIDTH + ATTN_WIDTH
IN_PROJ_WIDTH = 3 * CONV_WIDTH + ATTN_WIDTH + 2 * KV_WIDTH
FFN_RES_SCALE = 0.5
RMS_EPS = 1e-5
MASK_VALUE = -1e30

LANES = 128
BF16_SUBLANES = 16
FFN_TILE = 512
MIX_TILE = 512
FFN1_CHUNK = 1024
FFN2_CHUNK = 512
SCORE_LOOKAHEAD = 2
VMEM_LIMIT_BYTES = 56 * 1024 * 1024

_F32 = jnp.float32
_BF16 = jnp.bfloat16


def _rms_norm(x, gain):
    inv = lax.rsqrt(jnp.mean(x * x, axis=-1, keepdims=True) + RMS_EPS)
    return (x * inv) * gain


def _ffn_dots(h, wg_ref, wu_ref, wd_ref, chunk, defer_down, result):
    acc = None

    def down(g, u, c0, c1):
        a = (g * jax.nn.sigmoid(g) * u).astype(_BF16)
        part = jnp.dot(a, wd_ref[c0:c1, :], preferred_element_type=_F32)
        result["acc"] = part if acc is None else acc + part
        return result["acc"]

    pending = None
    for c0 in range(0, D_FF, chunk):
        c1 = min(c0 + chunk, D_FF)
        g = jnp.dot(h, wg_ref[:, c0:c1], preferred_element_type=_F32)
        yield
        u = jnp.dot(h, wu_ref[:, c0:c1], preferred_element_type=_F32)
        yield
        if pending is not None:
            acc = down(*pending)
            yield
        pending = (g, u, c0, c1)
        if not defer_down:
            acc = down(*pending)
            pending = None
            yield
    if pending is not None:
        acc = down(*pending)
        yield


def _swiglu_residual(x, gain, wg_ref, wu_ref, wd_ref):
    result = {}
    for _ in _ffn_dots(_rms_norm(x, gain).astype(_BF16), wg_ref, wu_ref, wd_ref, FFN1_CHUNK, True, result):
        pass
    return x + FFN_RES_SCALE * result["acc"]


def _rotate_head_pair(t, cos, sin_lo, sin_hi):
    return t * cos + pltpu.roll(t, LANES - ROT_HALF, 1) * sin_lo + pltpu.roll(t, ROT_HALF, 1) * sin_hi


def _rotary_coefficients(tile_cs, row_tab):
    ca, sa = tile_cs[:, :LANES], tile_cs[:, LANES:]
    tab = [row_tab[:, j * LANES:(j + 1) * LANES] for j in range(6)]
    return ca * tab[0] - sa * tab[1], sa * tab[2] + ca * tab[3], sa * tab[4] + ca * tab[5]


def _inproj_pieces(h, win_ref, tile_cs, row_tab, bg_ref, cu_ref, q_ref, kv_ref):
    def project(c0, width):
        return jnp.dot(h, win_ref[:, c0:c0 + width], preferred_element_type=_F32)

    cos, sin_lo, sin_hi = _rotary_coefficients(tile_cs, row_tab)
    q = project(3 * CONV_WIDTH, ATTN_WIDTH)
    for j in range(ATTN_WIDTH // LANES):
        q_ref[:, j * LANES:(j + 1) * LANES] = (
            _rotate_head_pair(q[:, j * LANES:(j + 1) * LANES], cos, sin_lo, sin_hi) * (HEAD_DIM ** -0.5)
        ).astype(_BF16)
    yield
    kv = project(3 * CONV_WIDTH + ATTN_WIDTH, 2 * KV_WIDTH)
    kv_ref[:, :KV_WIDTH] = _rotate_head_pair(kv[:, :KV_WIDTH], cos, sin_lo, sin_hi).astype(_BF16)
    kv_ref[:, KV_WIDTH:] = kv[:, KV_WIDTH:].astype(_BF16)
    yield
    bg_ref[...] = project(0, CONV_WIDTH).astype(_BF16)
    yield
    c_gate = project(CONV_WIDTH, CONV_WIDTH)
    yield
    cu_ref[...] = (c_gate * project(2 * CONV_WIDTH, CONV_WIDTH)).astype(_BF16)
    yield


def _ffn_inproj_kernel(x_ref, g1_ref, wg_ref, wu_ref, wd_ref, g2_ref, win_ref, tile_cs_ref, row_tab_ref,
                       x1_ref, bg_ref, cu_ref, q_ref, kv_ref, h_ref):
    step = pl.program_id(0)

    @pl.when(step == 0)
    def _():
        h_ref[1] = jnp.zeros(h_ref.shape[1:], h_ref.dtype)

    inproj = _inproj_pieces(h_ref[(step + 1) % 2], win_ref, tile_cs_ref[0], row_tab_ref[...],
                            bg_ref, cu_ref, q_ref, kv_ref)
    next(inproj)
    x1 = _swiglu_residual(x_ref[...], g1_ref[...], wg_ref, wu_ref, wd_ref)
    x1_ref[...] = x1
    h_ref[step % 2] = _rms_norm(x1, g2_ref[...]).astype(_BF16)
    for _ in inproj:
        pass


def _conv_mixer(first_tile, bg_ref, cu_ref, cu_halo_ref, convw_ref):
    rows = cu_ref.shape[0]
    v = cu_ref[...].astype(_F32)
    halo = jnp.where(first_tile, 0.0, cu_halo_ref[...].astype(_F32))
    prev1 = halo[BF16_SUBLANES - 1:BF16_SUBLANES, :]
    prev2 = halo[BF16_SUBLANES - 2:BF16_SUBLANES - 1, :]
    row = lax.broadcasted_iota(jnp.int32, (rows, 1), 0)
    v1 = jnp.where(row == 0, prev1, pltpu.roll(v, 1, 0))
    v2 = jnp.where(row == 0, prev2, jnp.where(row == 1, prev1, pltpu.roll(v, 2, 0)))
    w = convw_ref[...]
    return bg_ref[...].astype(_F32) * (w[0:1, :] * v2 + w[1:2, :] * v1 + w[2:3, :] * v)


def _attention_pieces(first_tile, sinks_ref, q_ref, kv_ref, kv_halo_ref, result):
    rows = q_ref.shape[0]
    kv_ext = jnp.concatenate([kv_halo_ref[...], kv_ref[...]], axis=0)
    qi = lax.broadcasted_iota(jnp.int32, (BLOCK, 2 * BLOCK), 0)
    kj = lax.broadcasted_iota(jnp.int32, (BLOCK, 2 * BLOCK), 1)
    in_window = (kj > qi) & (kj <= qi + BLOCK)
    first_mask = in_window & ((kj >= BLOCK) | jnp.logical_not(first_tile))

    def scores(blk, hkv):
        r0 = blk * BLOCK
        k = kv_ext[r0:r0 + 2 * BLOCK, hkv * HEAD_DIM:(hkv + 1) * HEAD_DIM]
        out = []
        for g in range(Q_PER_KV):
            hq = hkv * Q_PER_KV + g
            q = q_ref[r0:r0 + BLOCK, hq * HEAD_DIM:(hq + 1) * HEAD_DIM]
            out.append(lax.dot_general(q, k, (((1,), (1,)), ((), ())), preferred_element_type=_F32))
        return out

    def weighted_values(blk, hkv, score_list):
        r0 = blk * BLOCK
        vv = kv_ext[r0:r0 + 2 * BLOCK, KV_WIDTH + hkv * HEAD_DIM:KV_WIDTH + (hkv + 1) * HEAD_DIM]
        mask = in_window if blk > 0 else first_mask
        out = []
        for g, s in enumerate(score_list):
            sink = sinks_ref[hkv * Q_PER_KV + g]
            s = jnp.where(mask, s, MASK_VALUE)
            m = jnp.maximum(jnp.max(s, axis=-1, keepdims=True), sink)
            p = jnp.exp(s - m)
            denom = jnp.sum(p, axis=-1, keepdims=True) + jnp.exp(sink - m)
            out.append(jnp.dot(p.astype(_BF16), vv, preferred_element_type=_F32) / denom)
        return out

    pieces = [(blk, hkv) for blk in range(rows // BLOCK) for hkv in range(N_KV_HEADS)]
    pending = [scores(*piece) for piece in pieces[:SCORE_LOOKAHEAD]]
    yield
    blocks, heads = [], []
    for i, piece in enumerate(pieces):
        heads.extend(weighted_values(*piece, pending.pop(0)))
        if len(heads) == N_Q_HEADS:
            blocks.append(jnp.concatenate(heads, axis=1))
            heads = []
        if i + SCORE_LOOKAHEAD < len(pieces):
            pending.append(scores(*pieces[i + SCORE_LOOKAHEAD]))
        if i + 1 < len(pieces):
            yield
    result["y_attn"] = jnp.concatenate(blocks, axis=0)


def _mixer_ffn_kernel(sinks_ref, x1_ref, bg_ref, cu_ref, cu_halo_ref, q_ref, kv_ref, kv_halo_ref, convw_ref, wout_ref,
                      g_ref, wg_ref, wu_ref, wd_ref, gf_ref, out_ref, x2_ref, h_ref, *, tiles_per_seq):
    step = pl.program_id(0)

    @pl.when(step == 0)
    def _():
        x2_ref[1] = jnp.zeros(x2_ref.shape[1:], x2_ref.dtype)
        h_ref[1] = jnp.zeros(h_ref.shape[1:], h_ref.dtype)

    read_slot = (step + 1) % 2
    write_slot = step % 2
    first_tile = step % tiles_per_seq == 0
    ffn, att = {}, {}
    ffn_dots = _ffn_dots(h_ref[read_slot], wg_ref, wu_ref, wd_ref, FFN2_CHUNK, False, ffn)
    attention = _attention_pieces(first_tile, sinks_ref, q_ref, kv_ref, kv_halo_ref, att)
    next(ffn_dots)
    y_conv = _conv_mixer(first_tile, bg_ref, cu_ref, cu_halo_ref, convw_ref)
    for _ in attention:
        next(ffn_dots, None)
    y = jnp.concatenate([y_conv, att["y_attn"]], axis=1).astype(_BF16)
    x2 = x1_ref[...] + jnp.dot(y, wout_ref[...], preferred_element_type=_F32)
    x2_ref[write_slot] = x2
    h_ref[write_slot] = _rms_norm(x2, g_ref[...]).astype(_BF16)
    for _ in ffn_dots:
        pass
    x = x2_ref[read_slot] + FFN_RES_SCALE * ffn["acc"]
    out_ref[...] = _rms_norm(x, gf_ref[...])


def _resident(shape):
    return pl.BlockSpec(shape, lambda *_: (0,) * len(shape), pipeline_mode=pl.Buffered(1))


def _rotary_tables(seq, tile):
    inv_freq = ROPE_THETA ** (-jnp.arange(0, ROT_DIM, 2, dtype=_F32) / ROT_DIM)
    dim = jnp.arange(LANES) % HEAD_DIM
    freq = jnp.where(dim < ROT_DIM, inv_freq[dim % ROT_HALF], 0.0)
    lo = (dim < ROT_HALF).astype(_F32)
    hi = ((dim >= ROT_HALF) & (dim < ROT_DIM)).astype(_F32)
    start = (jnp.arange(seq // tile) * tile).astype(_F32)[:, None] * freq[None, :]
    row = jnp.arange(tile, dtype=_F32)[:, None] * freq[None, :]
    tile_cs = jnp.concatenate([jnp.cos(start), jnp.sin(start)], axis=1).reshape(seq // tile, 1, 2 * LANES)
    cb, sb = jnp.cos(row), jnp.sin(row)
    return tile_cs, jnp.concatenate([cb, sb, -cb * lo, -sb * lo, cb * hi, sb * hi], axis=1)


_FFN_WEIGHT_SPECS = [_resident((D_MODEL, D_FF)), _resident((D_MODEL, D_FF)), _resident((D_FF, D_MODEL))]
_PARAMS = functools.partial(pltpu.CompilerParams, vmem_limit_bytes=VMEM_LIMIT_BYTES)


def _ffn_inproj_call(xt, gain1, wg, wu, wd, gain2, win, seq):
    tokens = xt.shape[0]
    n_tiles = tokens // FFN_TILE
    seq_tiles = seq // FFN_TILE

    def ffn_spec(width):
        return pl.BlockSpec((FFN_TILE, width), lambda s: (jnp.minimum(s, n_tiles - 1), 0))

    def proj_spec(width):
        return pl.BlockSpec((FFN_TILE, width), lambda s: (jnp.maximum(s - 1, 0), 0))

    return pl.pallas_call(
        _ffn_inproj_kernel,
        grid=(n_tiles + 1,),
        in_specs=[ffn_spec(D_MODEL), _resident((1, D_MODEL)), *_FFN_WEIGHT_SPECS,
                  _resident((1, D_MODEL)), _resident((D_MODEL, IN_PROJ_WIDTH)),
                  pl.BlockSpec((1, 1, 2 * LANES), lambda s: (jnp.maximum(s - 1, 0) % seq_tiles, 0, 0)),
                  _resident((FFN_TILE, 6 * LANES))],
        out_specs=[ffn_spec(D_MODEL), proj_spec(CONV_WIDTH), proj_spec(CONV_WIDTH),
                   proj_spec(ATTN_WIDTH), proj_spec(2 * KV_WIDTH)],
        out_shape=[jax.ShapeDtypeStruct((tokens, D_MODEL), _F32),
                   jax.ShapeDtypeStruct((tokens, CONV_WIDTH), _BF16),
                   jax.ShapeDtypeStruct((tokens, CONV_WIDTH), _BF16),
                   jax.ShapeDtypeStruct((tokens, ATTN_WIDTH), _BF16),
                   jax.ShapeDtypeStruct((tokens, 2 * KV_WIDTH), _BF16)],
        scratch_shapes=[pltpu.VMEM((2, FFN_TILE, D_MODEL), _BF16)],
        compiler_params=_PARAMS(dimension_semantics=("arbitrary",)),
        name="ffn1_inproj",
    )(xt, gain1, wg, wu, wd, gain2, win, *_rotary_tables(seq, FFN_TILE))


def _mixer_ffn_call(sinks, x1, bg, cu, q, kv, conv_w, wout, gain, wg, wu, wd, final_gain, seq):
    tokens = x1.shape[0]
    n_tiles = tokens // MIX_TILE
    halo_per_tile = MIX_TILE // BF16_SUBLANES
    blocks_per_tile = MIX_TILE // BLOCK

    def mix_tile(s):
        return jnp.minimum(s, n_tiles - 1)

    def tile_spec(width):
        return pl.BlockSpec((MIX_TILE, width), lambda s, sinks: (mix_tile(s), 0))

    return pl.pallas_call(
        functools.partial(_mixer_ffn_kernel, tiles_per_seq=seq // MIX_TILE),
        grid_spec=pltpu.PrefetchScalarGridSpec(
            num_scalar_prefetch=1,
            grid=(n_tiles + 1,),
            in_specs=[
                tile_spec(D_MODEL), tile_spec(CONV_WIDTH), tile_spec(CONV_WIDTH),
                pl.BlockSpec((BF16_SUBLANES, CONV_WIDTH),
                             lambda s, sinks: (jnp.maximum(mix_tile(s) * halo_per_tile - 1, 0), 0)),
                tile_spec(ATTN_WIDTH), tile_spec(2 * KV_WIDTH),
                pl.BlockSpec((BLOCK, 2 * KV_WIDTH),
                             lambda s, sinks: (jnp.maximum(mix_tile(s) * blocks_per_tile - 1, 0), 0)),
                _resident((CONV_KERNEL, CONV_WIDTH)), _resident((MIX_WIDTH, D_MODEL)),
                _resident((1, D_MODEL)), *_FFN_WEIGHT_SPECS, _resident((1, D_MODEL)),
            ],
            out_specs=pl.BlockSpec((MIX_TILE, D_MODEL), lambda s, sinks: (jnp.maximum(s - 1, 0), 0)),
            scratch_shapes=[pltpu.VMEM((2, MIX_TILE, D_MODEL), _F32), pltpu.VMEM((2, MIX_TILE, D_MODEL), _BF16)],
        ),
        out_shape=jax.ShapeDtypeStruct((tokens, D_MODEL), _F32),
        compiler_params=_PARAMS(dimension_semantics=("arbitrary",)),
        name="mixer_ffn2",
    )(sinks, x1, bg, cu, cu, q, kv, kv, conv_w, wout, gain, wg, wu, wd, final_gain)


def kernel(x, ffn1_norm, ffn1_w_gate, ffn1_w_up, ffn1_w_down, mix_norm, w_in, conv_w, attn_sinks, w_out,
           ffn2_norm, ffn2_w_gate, ffn2_w_up, ffn2_w_down, final_norm):
    batch, seq, d_model = x.shape
    assert d_model == D_MODEL and ffn1_norm.shape[0] == 1
    assert seq % MIX_TILE == 0 and seq % FFN_TILE == 0
    xt = x.reshape(batch * seq, D_MODEL)
    x1, bg, cu, q, kv = _ffn_inproj_call(
        xt, ffn1_norm, ffn1_w_gate[0].astype(_BF16), ffn1_w_up[0].astype(_BF16), ffn1_w_down[0].astype(_BF16),
        mix_norm, w_in[0].astype(_BF16), seq)
    out = _mixer_ffn_call(
        attn_sinks[0], x1, bg, cu, q, kv, conv_w[0], w_out[0].astype(_BF16),
        ffn2_norm, ffn2_w_gate[0].astype(_BF16), ffn2_w_up[0].astype(_BF16), ffn2_w_down[0].astype(_BF16),
        final_norm.reshape(1, D_MODEL), seq)
    return out.reshape(batch, seq, D_MODEL)
```

```python
import functools

import jax
import jax.numpy as jnp
from jax import lax
from jax.experimental import pallas as pl
from jax.experimental.pallas import tpu as pltpu

D_MODEL = 1024
D_FF = 2816
CONV_WIDTH = 512
CONV_KERNEL = 3
N_Q_HEADS = 8
N_KV_HEADS = 2
Q_PER_KV = N_Q_HEADS // N_KV_HEADS
HEAD_DIM = 64
ATTN_WIDTH = N_Q_HEADS * HEAD_DIM
KV_WIDTH = N_KV_HEADS * HEAD_DIM
WINDOW = 128
BLOCK = 128
ROPE_THETA = 500000.0
ROT_DIM = HEAD_DIM // 4
ROT_HALF = ROT_DIM // 2
MIX_WIDTH = CONV_WIDTH + ATTN_WIDTH
IN_PROJ_WIDTH = 3 * CONV_WIDTH + ATTN_WIDTH + 2 * KV_WIDTH
FFN_RES_SCALE = 0.5
RMS_EPS = 1e-5
MASK_VALUE = -1e30

LANES = 128
BF16_SUBLANES = 16
FFN_TILE = 512
MIX_TILE = 512
FFN1_CHUNK = 1024
FFN2_CHUNK = 512
SCORE_LOOKAHEAD = 2
VMEM_LIMIT_BYTES = 56 * 1024 * 1024

_F32 = jnp.float32
_BF16 = jnp.bfloat16


def _rms_norm(x, gain):
    inv = lax.rsqrt(jnp.mean(x * x, axis=-1, keepdims=True) + RMS_EPS)
    return (x * inv) * gain


def _ffn_dots(h, wg_ref, wu_ref, wd_ref, chunk, defer_down, result):
    acc = None

    def down(g, u, c0, c1):
        a = (g * jax.nn.sigmoid(g) * u).astype(_BF16)
        part = jnp.dot(a, wd_ref[c0:c1, :], preferred_element_type=_F32)
        result["acc"] = part if acc is None else acc + part
        return result["acc"]

    pending = None
    for c0 in range(0, D_FF, chunk):
        c1 = min(c0 + chunk, D_FF)
        g = jnp.dot(h, wg_ref[:, c0:c1], preferred_element_type=_F32)
        yield
        u = jnp.dot(h, wu_ref[:, c0:c1], preferred_element_type=_F32)
        yield
        if pending is not None:
            acc = down(*pending)
            yield
        pending = (g, u, c0, c1)
        if not defer_down:
            acc = down(*pending)
            pending = None
            yield
    if pending is not None:
        acc = down(*pending)
        yield


def _swiglu_residual(x, gain, wg_ref, wu_ref, wd_ref):
    result = {}
    for _ in _ffn_dots(_rms_norm(x, gain).astype(_BF16), wg_ref, wu_ref, wd_ref, FFN1_CHUNK, True, result):
        pass
    return x + FFN_RES_SCALE * result["acc"]


def _rotate_head_pair(t, cos, sin_lo, sin_hi):
    return t * cos + pltpu.roll(t, LANES - ROT_HALF, 1) * sin_lo + pltpu.roll(t, ROT_HALF, 1) * sin_hi


def _rotary_coefficients(tile_cs, row_tab):
    ca, sa = tile_cs[:, :LANES], tile_cs[:, LANES:]
    tab = [row_tab[:, j * LANES:(j + 1) * LANES] for j in range(6)]
    return ca * tab[0] - sa * tab[1], sa * tab[2] + ca * tab[3], sa * tab[4] + ca * tab[5]


def _inproj_pieces(h, win_ref, tile_cs, row_tab, bg_ref, cu_ref, q_ref, kv_ref):
    def project(c0, width):
        return jnp.dot(h, win_ref[:, c0:c0 + width], preferred_element_type=_F32)

    cos, sin_lo, sin_hi = _rotary_coefficients(tile_cs, row_tab)
    q = project(3 * CONV_WIDTH, ATTN_WIDTH)
    for j in range(ATTN_WIDTH // LANES):
        q_ref[:, j * LANES:(j + 1) * LANES] = (
            _rotate_head_pair(q[:, j * LANES:(j + 1) * LANES], cos, sin_lo, sin_hi) * (HEAD_DIM ** -0.5)
        ).astype(_BF16)
    yield
    kv = project(3 * CONV_WIDTH + ATTN_WIDTH, 2 * KV_WIDTH)
    kv_ref[:, :KV_WIDTH] = _rotate_head_pair(kv[:, :KV_WIDTH], cos, sin_lo, sin_hi).astype(_BF16)
    kv_ref[:, KV_WIDTH:] = kv[:, KV_WIDTH:].astype(_BF16)
    yield
    bg_ref[...] = project(0, CONV_WIDTH).astype(_BF16)
    yield
    c_gate = project(CONV_WIDTH, CONV_WIDTH)
    yield
    cu_ref[...] = (c_gate * project(2 * CONV_WIDTH, CONV_WIDTH)).astype(_BF16)
    yield


def _ffn_inproj_kernel(x_ref, g1_ref, wg_ref, wu_ref, wd_ref, g2_ref, win_ref, tile_cs_ref, row_tab_ref,
                       wout_f32_ref, wg2_f32_ref, wu2_f32_ref, wd2_f32_ref,
                       x1_ref, bg_ref, cu_ref, q_ref, kv_ref, wout_ref, wg2_ref, wu2_ref, wd2_ref, h_ref):
    for src_ref, dst_ref in ((wout_f32_ref, wout_ref), (wg2_f32_ref, wg2_ref), (wu2_f32_ref, wu2_ref),
                             (wd2_f32_ref, wd2_ref)):
        dst_ref[...] = src_ref[...].astype(_BF16)
    step = pl.program_id(0)

    @pl.when(step == 0)
    def _():
        h_ref[1] = jnp.zeros(h_ref.shape[1:], h_ref.dtype)

    inproj = _inproj_pieces(h_ref[(step + 1) % 2], win_ref, tile_cs_ref[0], row_tab_ref[...],
                            bg_ref, cu_ref, q_ref, kv_ref)
    next(inproj)
    x1 = _swiglu_residual(x_ref[...], g1_ref[...], wg_ref, wu_ref, wd_ref)
    x1_ref[...] = x1
    h_ref[step % 2] = _rms_norm(x1, g2_ref[...]).astype(_BF16)
    for _ in inproj:
        pass


def _conv_mixer(first_tile, bg_ref, cu_ref, cu_halo_ref, convw_ref):
    rows = cu_ref.shape[0]
    v = cu_ref[...].astype(_F32)
    halo = jnp.where(first_tile, 0.0, cu_halo_ref[...].astype(_F32))
    prev1 = halo[BF16_SUBLANES - 1:BF16_SUBLANES, :]
    prev2 = halo[BF16_SUBLANES - 2:BF16_SUBLANES - 1, :]
    row = lax.broadcasted_iota(jnp.int32, (rows, 1), 0)
    v1 = jnp.where(row == 0, prev1, pltpu.roll(v, 1, 0))
    v2 = jnp.where(row == 0, prev2, jnp.where(row == 1, prev1, pltpu.roll(v, 2, 0)))
    w = convw_ref[...]
    return bg_ref[...].astype(_F32) * (w[0:1, :] * v2 + w[1:2, :] * v1 + w[2:3, :] * v)


def _attention_pieces(first_tile, sinks_ref, q_ref, kv_ref, kv_halo_ref, result):
    rows = q_ref.shape[0]
    kv_ext = jnp.concatenate([kv_halo_ref[...], kv_ref[...]], axis=0)
    qi = lax.broadcasted_iota(jnp.int32, (BLOCK, 2 * BLOCK), 0)
    kj = lax.broadcasted_iota(jnp.int32, (BLOCK, 2 * BLOCK), 1)
    in_window = (kj > qi) & (kj <= qi + BLOCK)
    first_mask = in_window & ((kj >= BLOCK) | jnp.logical_not(first_tile))

    def scores(blk, hkv):
        r0 = blk * BLOCK
        k = kv_ext[r0:r0 + 2 * BLOCK, hkv * HEAD_DIM:(hkv + 1) * HEAD_DIM]
        out = []
        for g in range(Q_PER_KV):
            hq = hkv * Q_PER_KV + g
            q = q_ref[r0:r0 + BLOCK, hq * HEAD_DIM:(hq + 1) * HEAD_DIM]
            out.append(lax.dot_general(q, k, (((1,), (1,)), ((), ())), preferred_element_type=_F32))
        return out

    def weighted_values(blk, hkv, score_list):
        r0 = blk * BLOCK
        vv = kv_ext[r0:r0 + 2 * BLOCK, KV_WIDTH + hkv * HEAD_DIM:KV_WIDTH + (hkv + 1) * HEAD_DIM]
        mask = in_window if blk > 0 else first_mask
        out = []
        for g, s in enumerate(score_list):
            sink = sinks_ref[hkv * Q_PER_KV + g]
            s = jnp.where(mask, s, MASK_VALUE)
            m = jnp.maximum(jnp.max(s, axis=-1, keepdims=True), sink)
            p = jnp.exp(s - m)
            denom = jnp.sum(p, axis=-1, keepdims=True) + jnp.exp(sink - m)
            out.append(jnp.dot(p.astype(_BF16), vv, preferred_element_type=_F32) / denom)
        return out

    pieces = [(blk, hkv) for blk in range(rows // BLOCK) for hkv in range(N_KV_HEADS)]
    pending = [scores(*piece) for piece in pieces[:SCORE_LOOKAHEAD]]
    yield
    blocks, heads = [], []
    for i, piece in enumerate(pieces):
        heads.extend(weighted_values(*piece, pending.pop(0)))
        if len(heads) == N_Q_HEADS:
            blocks.append(jnp.concatenate(heads, axis=1))
            heads = []
        if i + SCORE_LOOKAHEAD < len(pieces):
            pending.append(scores(*pieces[i + SCORE_LOOKAHEAD]))
        if i + 1 < len(pieces):
            yield
    result["y_attn"] = jnp.concatenate(blocks, axis=0)


def _mixer_ffn_kernel(sinks_ref, x1_ref, bg_ref, cu_ref, cu_halo_ref, q_ref, kv_ref, kv_halo_ref, convw_ref, wout_ref,
                      g_ref, wg_ref, wu_ref, wd_ref, gf_ref, out_ref, x2_ref, h_ref, *, tiles_per_seq):
    step = pl.program_id(0)

    @pl.when(step == 0)
    def _():
        x2_ref[1] = jnp.zeros(x2_ref.shape[1:], x2_ref.dtype)
        h_ref[1] = jnp.zeros(h_ref.shape[1:], h_ref.dtype)

    read_slot = (step + 1) % 2
    write_slot = step % 2
    first_tile = step % tiles_per_seq == 0
    ffn, att = {}, {}
    ffn_dots = _ffn_dots(h_ref[read_slot], wg_ref, wu_ref, wd_ref, FFN2_CHUNK, False, ffn)
    attention = _attention_pieces(first_tile, sinks_ref, q_ref, kv_ref, kv_halo_ref, att)
    next(ffn_dots)
    y_conv = _conv_mixer(first_tile, bg_ref, cu_ref, cu_halo_ref, convw_ref)
    for _ in attention:
        next(ffn_dots, None)
    y = jnp.concatenate([y_conv, att["y_attn"]], axis=1).astype(_BF16)
    x2 = x1_ref[...] + jnp.dot(y, wout_ref[...], preferred_element_type=_F32)
    x2_ref[write_slot] = x2
    h_ref[write_slot] = _rms_norm(x2, g_ref[...]).astype(_BF16)
    for _ in ffn_dots:
        pass
    x = x2_ref[read_slot] + FFN_RES_SCALE * ffn["acc"]
    out_ref[...] = _rms_norm(x, gf_ref[...])


def _resident(shape):
    return pl.BlockSpec(shape, lambda *_: (0,) * len(shape), pipeline_mode=pl.Buffered(1))


def _rotary_tables(seq, tile):
    inv_freq = ROPE_THETA ** (-jnp.arange(0, ROT_DIM, 2, dtype=_F32) / ROT_DIM)
    dim = jnp.arange(LANES) % HEAD_DIM
    freq = jnp.where(dim < ROT_DIM, inv_freq[dim % ROT_HALF], 0.0)
    lo = (dim < ROT_HALF).astype(_F32)
    hi = ((dim >= ROT_HALF) & (dim < ROT_DIM)).astype(_F32)
    start = (jnp.arange(seq // tile) * tile).astype(_F32)[:, None] * freq[None, :]
    row = jnp.arange(tile, dtype=_F32)[:, None] * freq[None, :]
    tile_cs = jnp.concatenate([jnp.cos(start), jnp.sin(start)], axis=1).reshape(seq // tile, 1, 2 * LANES)
    cb, sb = jnp.cos(row), jnp.sin(row)
    return tile_cs, jnp.concatenate([cb, sb, -cb * lo, -sb * lo, cb * hi, sb * hi], axis=1)


_FFN_WEIGHT_SPECS = [_resident((D_MODEL, D_FF)), _resident((D_MODEL, D_FF)), _resident((D_FF, D_MODEL))]
_PARAMS = functools.partial(pltpu.CompilerParams, vmem_limit_bytes=VMEM_LIMIT_BYTES)


def _cast_block_rows(rows, steps):
    return next(r for r in range(BF16_SUBLANES, rows + 1, BF16_SUBLANES) if rows % r == 0 and r * steps >= rows)


def _ffn_inproj_call(xt, gain1, wg, wu, wd, gain2, win, seq, later_weights):
    tokens = xt.shape[0]
    n_tiles = tokens // FFN_TILE
    seq_tiles = seq // FFN_TILE
    cast_specs, cast_shapes = [], []
    for w in later_weights:
        rows, cols = w.shape
        block_rows = _cast_block_rows(rows, n_tiles)
        index_map = functools.partial(lambda s, last: (jnp.minimum(s, last), 0), last=rows // block_rows - 1)
        cast_specs.append(pl.BlockSpec((block_rows, cols), index_map))
        cast_shapes.append(jax.ShapeDtypeStruct(w.shape, _BF16))

    def ffn_spec(width):
        return pl.BlockSpec((FFN_TILE, width), lambda s: (jnp.minimum(s, n_tiles - 1), 0))

    def proj_spec(width):
        return pl.BlockSpec((FFN_TILE, width), lambda s: (jnp.maximum(s - 1, 0), 0))

    return pl.pallas_call(
        _ffn_inproj_kernel,
        grid=(n_tiles + 1,),
        in_specs=[ffn_spec(D_MODEL), _resident((1, D_MODEL)), *_FFN_WEIGHT_SPECS,
                  _resident((1, D_MODEL)), _resident((D_MODEL, IN_PROJ_WIDTH)),
                  pl.BlockSpec((1, 1, 2 * LANES), lambda s: (jnp.maximum(s - 1, 0) % seq_tiles, 0, 0)),
                  _resident((FFN_TILE, 6 * LANES)), *cast_specs],
        out_specs=[ffn_spec(D_MODEL), proj_spec(CONV_WIDTH), proj_spec(CONV_WIDTH),
                   proj_spec(ATTN_WIDTH), proj_spec(2 * KV_WIDTH), *cast_specs],
        out_shape=[jax.ShapeDtypeStruct((tokens, D_MODEL), _F32),
                   jax.ShapeDtypeStruct((tokens, CONV_WIDTH), _BF16),
                   jax.ShapeDtypeStruct((tokens, CONV_WIDTH), _BF16),
                   jax.ShapeDtypeStruct((tokens, ATTN_WIDTH), _BF16),
                   jax.ShapeDtypeStruct((tokens, 2 * KV_WIDTH), _BF16), *cast_shapes],
        scratch_shapes=[pltpu.VMEM((2, FFN_TILE, D_MODEL), _BF16)],
        compiler_params=_PARAMS(dimension_semantics=("arbitrary",)),
        name="ffn1_inproj",
    )(xt, gain1, wg, wu, wd, gain2, win, *_rotary_tables(seq, FFN_TILE), *later_weights)


def _mixer_ffn_call(sinks, x1, bg, cu, q, kv, conv_w, wout, gain, wg, wu, wd, final_gain, seq):
    tokens = x1.shape[0]
    n_tiles = tokens // MIX_TILE
    halo_per_tile = MIX_TILE // BF16_SUBLANES
    blocks_per_tile = MIX_TILE // BLOCK

    def mix_tile(s):
        return jnp.minimum(s, n_tiles - 1)

    def tile_spec(width):
        return pl.BlockSpec((MIX_TILE, width), lambda s, sinks: (mix_tile(s), 0))

    return pl.pallas_call(
        functools.partial(_mixer_ffn_kernel, tiles_per_seq=seq // MIX_TILE),
        grid_spec=pltpu.PrefetchScalarGridSpec(
            num_scalar_prefetch=1,
            grid=(n_tiles + 1,),
            in_specs=[
                tile_spec(D_MODEL), tile_spec(CONV_WIDTH), tile_spec(CONV_WIDTH),
                pl.BlockSpec((BF16_SUBLANES, CONV_WIDTH),
                             lambda s, sinks: (jnp.maximum(mix_tile(s) * halo_per_tile - 1, 0), 0)),
                tile_spec(ATTN_WIDTH), tile_spec(2 * KV_WIDTH),
                pl.BlockSpec((BLOCK, 2 * KV_WIDTH),
                             lambda s, sinks: (jnp.maximum(mix_tile(s) * blocks_per_tile - 1, 0), 0)),
                _resident((CONV_KERNEL, CONV_WIDTH)), _resident((MIX_WIDTH, D_MODEL)),
                _resident((1, D_MODEL)), *_FFN_WEIGHT_SPECS, _resident((1, D_MODEL)),
            ],
            out_specs=pl.BlockSpec((MIX_TILE, D_MODEL), lambda s, sinks: (jnp.maximum(s - 1, 0), 0)),
            scratch_shapes=[pltpu.VMEM((2, MIX_TILE, D_MODEL), _F32), pltpu.VMEM((2, MIX_TILE, D_MODEL), _BF16)],
        ),
        out_shape=jax.ShapeDtypeStruct((tokens, D_MODEL), _F32),
        compiler_params=_PARAMS(dimension_semantics=("arbitrary",)),
        name="mixer_ffn2",
    )(sinks, x1, bg, cu, cu, q, kv, kv, conv_w, wout, gain, wg, wu, wd, final_gain)


def kernel(x, ffn1_norm, ffn1_w_gate, ffn1_w_up, ffn1_w_down, mix_norm, w_in, conv_w, attn_sinks, w_out,
           ffn2_norm, ffn2_w_gate, ffn2_w_up, ffn2_w_down, final_norm):
    batch, seq, d_model = x.shape
    assert d_model == D_MODEL and ffn1_norm.shape[0] == 1
    assert seq % MIX_TILE == 0 and seq % FFN_TILE == 0
    xt = x.reshape(batch * seq, D_MODEL)
    x1, bg, cu, q, kv, wout, wg2, wu2, wd2 = _ffn_inproj_call(
        xt, ffn1_norm, ffn1_w_gate[0].astype(_BF16), ffn1_w_up[0].astype(_BF16), ffn1_w_down[0].astype(_BF16),
        mix_norm, w_in[0].astype(_BF16), seq, (w_out[0], ffn2_w_gate[0], ffn2_w_up[0], ffn2_w_down[0]))
    out = _mixer_ffn_call(
        attn_sinks[0], x1, bg, cu, q, kv, conv_w[0], wout, ffn2_norm, wg2, wu2, wd2,
        final_norm.reshape(1, D_MODEL), seq)
    return out.reshape(batch, seq, D_MODEL)
```

```python
import functools

import jax
import jax.numpy as jnp
from jax import lax
from jax.experimental import pallas as pl
from jax.experimental.pallas import tpu as pltpu

D_MODEL = 1024
D_FF = 2816
CONV_WIDTH = 512
CONV_KERNEL = 3
N_Q_HEADS = 8
N_KV_HEADS = 2
Q_PER_KV = N_Q_HEADS // N_KV_HEADS
HEAD_DIM = 64
ATTN_WIDTH = N_Q_HEADS * HEAD_DIM
KV_WIDTH = N_KV_HEADS * HEAD_DIM
WINDOW = 128
BLOCK = 128
ROPE_THETA = 500000.0
ROT_DIM = HEAD_DIM // 4
ROT_HALF = ROT_DIM // 2
MIX_WIDTH = CONV_WIDTH + ATTN_WIDTH
IN_PROJ_WIDTH = 3 * CONV_WIDTH + ATTN_WIDTH + 2 * KV_WIDTH
Z_BG, Z_CU, Z_Q, Z_KV = 0, CONV_WIDTH, 2 * CONV_WIDTH, 2 * CONV_WIDTH + ATTN_WIDTH
Z_WIDTH = Z_KV + 2 * KV_WIDTH
FFN_RES_SCALE = 0.5
RMS_EPS = 1e-5
MASK_VALUE = -1e30

LANES = 128
BF16_SUBLANES = 16
FFN_TILE = 512
MIX_TILE = 512
FFN1_CHUNK = 512
FFN2_CHUNK = 512
SCORE_LOOKAHEAD = 2
VMEM_LIMIT_BYTES = 56 * 1024 * 1024

_F32 = jnp.float32
_BF16 = jnp.bfloat16


def _rms_norm(x, gain):
    inv = lax.rsqrt(jnp.mean(x * x, axis=-1, keepdims=True) + RMS_EPS)
    return (x * inv) * gain


def _ffn_dots(h, wg_ref, wu_ref, wd_ref, chunk, defer_down, result):
    acc = None

    def down(g, u, c0, c1):
        a = (g * jax.nn.sigmoid(g) * u).astype(_BF16)
        part = jnp.dot(a, wd_ref[c0:c1, :], preferred_element_type=_F32)
        result["acc"] = part if acc is None else acc + part
        return result["acc"]

    pending = None
    for c0 in range(0, D_FF, chunk):
        c1 = min(c0 + chunk, D_FF)
        g = jnp.dot(h, wg_ref[:, c0:c1], preferred_element_type=_F32)
        yield
        u = jnp.dot(h, wu_ref[:, c0:c1], preferred_element_type=_F32)
        yield
        if pending is not None:
            acc = down(*pending)
            yield
        pending = (g, u, c0, c1)
        if not defer_down:
            acc = down(*pending)
            pending = None
            yield
    if pending is not None:
        acc = down(*pending)
        yield


def _swiglu_residual(x, gain, wg_ref, wu_ref, wd_ref):
    result = {}
    for _ in _ffn_dots(_rms_norm(x, gain).astype(_BF16), wg_ref, wu_ref, wd_ref, FFN1_CHUNK, True, result):
        pass
    return x + FFN_RES_SCALE * result["acc"]


def _rotate_head_pair(t, cos, sin_lo, sin_hi):
    return t * cos + pltpu.roll(t, LANES - ROT_HALF, 1) * sin_lo + pltpu.roll(t, ROT_HALF, 1) * sin_hi


def _rotary_coefficients(tile_cs, row_tab):
    ca, sa = tile_cs[:, :LANES], tile_cs[:, LANES:]
    tab = [row_tab[:, j * LANES:(j + 1) * LANES] for j in range(6)]
    return ca * tab[0] - sa * tab[1], sa * tab[2] + ca * tab[3], sa * tab[4] + ca * tab[5]


def _mixer_inputs(z_ref):
    return (z_ref.at[:, pl.ds(Z_BG, CONV_WIDTH)], z_ref.at[:, pl.ds(Z_CU, CONV_WIDTH)],
            z_ref.at[:, pl.ds(Z_Q, ATTN_WIDTH)], z_ref.at[:, pl.ds(Z_KV, 2 * KV_WIDTH)])


def _inproj_pieces(h, win_ref, tile_cs, row_tab, bg_ref, cu_ref, q_ref, kv_ref):
    def project(c0, width):
        return jnp.dot(h, win_ref[:, c0:c0 + width], preferred_element_type=_F32)

    cos, sin_lo, sin_hi = _rotary_coefficients(tile_cs, row_tab)
    q = project(3 * CONV_WIDTH, ATTN_WIDTH)
    for j in range(ATTN_WIDTH // LANES):
        q_ref[:, j * LANES:(j + 1) * LANES] = (
            _rotate_head_pair(q[:, j * LANES:(j + 1) * LANES], cos, sin_lo, sin_hi) * (HEAD_DIM ** -0.5)
        ).astype(_BF16)
    yield
    kv = project(3 * CONV_WIDTH + ATTN_WIDTH, 2 * KV_WIDTH)
    kv_ref[:, :KV_WIDTH] = _rotate_head_pair(kv[:, :KV_WIDTH], cos, sin_lo, sin_hi).astype(_BF16)
    kv_ref[:, KV_WIDTH:] = kv[:, KV_WIDTH:].astype(_BF16)
    yield
    bg_ref[...] = project(0, CONV_WIDTH).astype(_BF16)
    yield
    c_gate = project(CONV_WIDTH, CONV_WIDTH)
    yield
    cu_ref[...] = (c_gate * project(2 * CONV_WIDTH, CONV_WIDTH)).astype(_BF16)
    yield


def _ffn_inproj_kernel(x_ref, g1_ref, wg_ref, wu_ref, wd_ref, g2_ref, win_ref, tile_cs_ref, row_tab_ref,
                       wout_f32_ref, wg2_f32_ref, wu2_f32_ref, wd2_f32_ref,
                       x1_ref, z_ref, wout_ref, wg2_ref, wu2_ref, wd2_ref, h_ref):
    for src_ref, dst_ref in ((wout_f32_ref, wout_ref), (wg2_f32_ref, wg2_ref), (wu2_f32_ref, wu2_ref),
                             (wd2_f32_ref, wd2_ref)):
        dst_ref[...] = src_ref[...].astype(_BF16)
    step = pl.program_id(0)

    @pl.when(step == 0)
    def _():
        h_ref[1] = jnp.zeros(h_ref.shape[1:], h_ref.dtype)

    inproj = _inproj_pieces(h_ref[(step + 1) % 2], win_ref, tile_cs_ref[0], row_tab_ref[...], *_mixer_inputs(z_ref))
    next(inproj)
    x1 = _swiglu_residual(x_ref[...], g1_ref[...], wg_ref, wu_ref, wd_ref)
    x1_ref[...] = x1
    h_ref[step % 2] = _rms_norm(x1, g2_ref[...]).astype(_BF16)
    for _ in inproj:
        pass


def _conv_mixer(first_tile, bg_ref, cu_ref, cu_halo_ref, convw_ref):
    rows = cu_ref.shape[0]
    v = cu_ref[...].astype(_F32)
    halo = jnp.where(first_tile, 0.0, cu_halo_ref[...].astype(_F32))
    prev1 = halo[BF16_SUBLANES - 1:BF16_SUBLANES, :]
    prev2 = halo[BF16_SUBLANES - 2:BF16_SUBLANES - 1, :]
    row = lax.broadcasted_iota(jnp.int32, (rows, 1), 0)
    v1 = jnp.where(row == 0, prev1, pltpu.roll(v, 1, 0))
    v2 = jnp.where(row == 0, prev2, jnp.where(row == 1, prev1, pltpu.roll(v, 2, 0)))
    w = convw_ref[...]
    return bg_ref[...].astype(_F32) * (w[0:1, :] * v2 + w[1:2, :] * v1 + w[2:3, :] * v)


def _attention_pieces(first_tile, sinks_ref, q_ref, kv_ref, kv_halo_ref, result):
    rows = q_ref.shape[0]
    kv_ext = jnp.concatenate([kv_halo_ref[...], kv_ref[...]], axis=0)
    stacked = Q_PER_KV * BLOCK
    row = lax.broadcasted_iota(jnp.int32, (stacked, 2 * BLOCK), 0)
    qi = row % BLOCK
    kj = lax.broadcasted_iota(jnp.int32, (stacked, 2 * BLOCK), 1)
    in_window = (kj > qi) & (kj <= qi + BLOCK)
    first_mask = in_window & ((kj >= BLOCK) | jnp.logical_not(first_tile))
    group = lax.broadcasted_iota(jnp.int32, (stacked, 1), 0) // BLOCK

    def sink_column(hkv):
        col = jnp.full((stacked, 1), sinks_ref[hkv * Q_PER_KV], _F32)
        for g in range(1, Q_PER_KV):
            col = jnp.where(group == g, sinks_ref[hkv * Q_PER_KV + g], col)
        return col

    sink_columns = [sink_column(hkv) for hkv in range(N_KV_HEADS)]

    def scores(blk, hkv):
        r0 = blk * BLOCK
        k = kv_ext[r0:r0 + 2 * BLOCK, hkv * HEAD_DIM:(hkv + 1) * HEAD_DIM]
        q = jnp.concatenate([q_ref[r0:r0 + BLOCK, hq * HEAD_DIM:(hq + 1) * HEAD_DIM]
                             for hq in range(hkv * Q_PER_KV, (hkv + 1) * Q_PER_KV)], axis=0)
        return lax.dot_general(q, k, (((1,), (1,)), ((), ())), preferred_element_type=_F32)

    def weighted_values(blk, hkv, s):
        r0 = blk * BLOCK
        vv = kv_ext[r0:r0 + 2 * BLOCK, KV_WIDTH + hkv * HEAD_DIM:KV_WIDTH + (hkv + 1) * HEAD_DIM]
        sink = sink_columns[hkv]
        s = jnp.where(in_window if blk > 0 else first_mask, s, MASK_VALUE)
        m = jnp.maximum(jnp.max(s, axis=-1, keepdims=True), sink)
        p = jnp.exp(s - m)
        denom = jnp.sum(p, axis=-1, keepdims=True) + jnp.exp(sink - m)
        o = jnp.dot(p.astype(_BF16), vv, preferred_element_type=_F32) / denom
        return [o[g * BLOCK:(g + 1) * BLOCK] for g in range(Q_PER_KV)]

    pieces = [(blk, hkv) for blk in range(rows // BLOCK) for hkv in range(N_KV_HEADS)]
    pending = [scores(*piece) for piece in pieces[:SCORE_LOOKAHEAD]]
    yield
    blocks, heads = [], []
    for i, piece in enumerate(pieces):
        heads.extend(weighted_values(*piece, pending.pop(0)))
        if len(heads) == N_Q_HEADS:
            blocks.append(jnp.concatenate(heads, axis=1))
            heads = []
        if i + SCORE_LOOKAHEAD < len(pieces):
            pending.append(scores(*pieces[i + SCORE_LOOKAHEAD]))
        if i + 1 < len(pieces):
            yield
    result["y_attn"] = jnp.concatenate(blocks, axis=0)


def _mixer_ffn_kernel(sinks_ref, x1_ref, z_ref, cu_halo_ref, kv_halo_ref, convw_ref, wout_ref,
                      g_ref, wg_ref, wu_ref, wd_ref, gf_ref, out_ref, x2_ref, h_ref, *, tiles_per_seq):
    step = pl.program_id(0)

    @pl.when(step == 0)
    def _():
        x2_ref[1] = jnp.zeros(x2_ref.shape[1:], x2_ref.dtype)
        h_ref[1] = jnp.zeros(h_ref.shape[1:], h_ref.dtype)

    bg_ref, cu_ref, q_ref, kv_ref = _mixer_inputs(z_ref)
    read_slot = (step + 1) % 2
    write_slot = step % 2
    first_tile = step % tiles_per_seq == 0
    ffn, att = {}, {}
    ffn_dots = _ffn_dots(h_ref[read_slot], wg_ref, wu_ref, wd_ref, FFN2_CHUNK, False, ffn)
    attention = _attention_pieces(first_tile, sinks_ref, q_ref, kv_ref, kv_halo_ref, att)
    next(ffn_dots)
    y_conv = _conv_mixer(first_tile, bg_ref, cu_ref, cu_halo_ref, convw_ref)
    for _ in attention:
        next(ffn_dots, None)
    y = jnp.concatenate([y_conv, att["y_attn"]], axis=1).astype(_BF16)
    x2 = x1_ref[...] + jnp.dot(y, wout_ref[...], preferred_element_type=_F32)
    x2_ref[write_slot] = x2
    h_ref[write_slot] = _rms_norm(x2, g_ref[...]).astype(_BF16)
    for _ in ffn_dots:
        pass
    x = x2_ref[read_slot] + FFN_RES_SCALE * ffn["acc"]
    out_ref[...] = _rms_norm(x, gf_ref[...])


def _resident(shape):
    return pl.BlockSpec(shape, lambda *_: (0,) * len(shape), pipeline_mode=pl.Buffered(1))


def _rotary_tables(seq, tile):
    inv_freq = ROPE_THETA ** (-jnp.arange(0, ROT_DIM, 2, dtype=_F32) / ROT_DIM)
    dim = jnp.arange(LANES) % HEAD_DIM
    freq = jnp.where(dim < ROT_DIM, inv_freq[dim % ROT_HALF], 0.0)
    lo = (dim < ROT_HALF).astype(_F32)
    hi = ((dim >= ROT_HALF) & (dim < ROT_DIM)).astype(_F32)
    start = (jnp.arange(seq // tile) * tile).astype(_F32)[:, None] * freq[None, :]
    row = jnp.arange(tile, dtype=_F32)[:, None] * freq[None, :]
    tile_cs = jnp.concatenate([jnp.cos(start), jnp.sin(start)], axis=1).reshape(seq // tile, 1, 2 * LANES)
    cb, sb = jnp.cos(row), jnp.sin(row)
    return tile_cs, jnp.concatenate([cb, sb, -cb * lo, -sb * lo, cb * hi, sb * hi], axis=1)


_FFN_WEIGHT_SPECS = [_resident((D_MODEL, D_FF)), _resident((D_MODEL, D_FF)), _resident((D_FF, D_MODEL))]
_PARAMS = functools.partial(pltpu.CompilerParams, vmem_limit_bytes=VMEM_LIMIT_BYTES)


def _cast_block_rows(rows, steps):
    return next(r for r in range(BF16_SUBLANES, rows + 1, BF16_SUBLANES) if rows % r == 0 and r * steps >= rows)


def _ffn_inproj_call(xt, gain1, wg, wu, wd, gain2, win, seq, later_weights):
    tokens = xt.shape[0]
    n_tiles = tokens // FFN_TILE
    seq_tiles = seq // FFN_TILE
    cast_specs, cast_shapes = [], []
    for w in later_weights:
        rows, cols = w.shape
        block_rows = _cast_block_rows(rows, n_tiles)
        index_map = functools.partial(lambda s, last: (jnp.minimum(s, last), 0), last=rows // block_rows - 1)
        cast_specs.append(pl.BlockSpec((block_rows, cols), index_map))
        cast_shapes.append(jax.ShapeDtypeStruct(w.shape, _BF16))

    def ffn_spec(width):
        return pl.BlockSpec((FFN_TILE, width), lambda s: (jnp.minimum(s, n_tiles - 1), 0))

    def proj_spec(width):
        return pl.BlockSpec((FFN_TILE, width), lambda s: (jnp.maximum(s - 1, 0), 0))

    return pl.pallas_call(
        _ffn_inproj_kernel,
        grid=(n_tiles + 1,),
        in_specs=[ffn_spec(D_MODEL), _resident((1, D_MODEL)), *_FFN_WEIGHT_SPECS,
                  _resident((1, D_MODEL)), _resident((D_MODEL, IN_PROJ_WIDTH)),
                  pl.BlockSpec((1, 1, 2 * LANES), lambda s: (jnp.maximum(s - 1, 0) % seq_tiles, 0, 0)),
                  _resident((FFN_TILE, 6 * LANES)), *cast_specs],
        out_specs=[ffn_spec(D_MODEL), proj_spec(Z_WIDTH), *cast_specs],
        out_shape=[jax.ShapeDtypeStruct((tokens, D_MODEL), _F32),
                   jax.ShapeDtypeStruct((tokens, Z_WIDTH), _BF16), *cast_shapes],
        scratch_shapes=[pltpu.VMEM((2, FFN_TILE, D_MODEL), _BF16)],
        compiler_params=_PARAMS(dimension_semantics=("arbitrary",)),
        name="ffn1_inproj",
    )(xt, gain1, wg, wu, wd, gain2, win, *_rotary_tables(seq, FFN_TILE), *later_weights)


def _mixer_ffn_call(sinks, x1, z, conv_w, wout, gain, wg, wu, wd, final_gain, seq):
    tokens = x1.shape[0]
    n_tiles = tokens // MIX_TILE
    halo_per_tile = MIX_TILE // BF16_SUBLANES
    blocks_per_tile = MIX_TILE // BLOCK

    def mix_tile(s):
        return jnp.minimum(s, n_tiles - 1)

    def tile_spec(width):
        return pl.BlockSpec((MIX_TILE, width), lambda s, sinks: (mix_tile(s), 0))

    return pl.pallas_call(
        functools.partial(_mixer_ffn_kernel, tiles_per_seq=seq // MIX_TILE),
        grid_spec=pltpu.PrefetchScalarGridSpec(
            num_scalar_prefetch=1,
            grid=(n_tiles + 1,),
            in_specs=[
                tile_spec(D_MODEL), tile_spec(Z_WIDTH),
                pl.BlockSpec((BF16_SUBLANES, CONV_WIDTH),
                             lambda s, sinks: (jnp.maximum(mix_tile(s) * halo_per_tile - 1, 0), Z_CU // CONV_WIDTH)),
                pl.BlockSpec((BLOCK, 2 * KV_WIDTH),
                             lambda s, sinks: (jnp.maximum(mix_tile(s) * blocks_per_tile - 1, 0),
                                               Z_KV // (2 * KV_WIDTH))),
                _resident((CONV_KERNEL, CONV_WIDTH)), _resident((MIX_WIDTH, D_MODEL)),
                _resident((1, D_MODEL)), *_FFN_WEIGHT_SPECS, _resident((1, D_MODEL)),
            ],
            out_specs=pl.BlockSpec((MIX_TILE, D_MODEL), lambda s, sinks: (jnp.maximum(s - 1, 0), 0)),
            scratch_shapes=[pltpu.VMEM((2, MIX_TILE, D_MODEL), _F32), pltpu.VMEM((2, MIX_TILE, D_MODEL), _BF16)],
        ),
        out_shape=jax.ShapeDtypeStruct((tokens, D_MODEL), _F32),
        compiler_params=_PARAMS(dimension_semantics=("arbitrary",)),
        name="mixer_ffn2",
    )(sinks, x1, z, z, z, conv_w, wout, gain, wg, wu, wd, final_gain)


def kernel(x, ffn1_norm, ffn1_w_gate, ffn1_w_up, ffn1_w_down, mix_norm, w_in, conv_w, attn_sinks, w_out,
           ffn2_norm, ffn2_w_gate, ffn2_w_up, ffn2_w_down, final_norm):
    batch, seq, d_model = x.shape
    assert d_model == D_MODEL and ffn1_norm.shape[0] == 1
    assert seq % MIX_TILE == 0 and seq % FFN_TILE == 0
    xt = x.reshape(batch * seq, D_MODEL)
    x1, z, wout, wg2, wu2, wd2 = _ffn_inproj_call(
        xt, ffn1_norm, ffn1_w_gate[0].astype(_BF16), ffn1_w_up[0].astype(_BF16), ffn1_w_down[0].astype(_BF16),
        mix_norm, w_in[0].astype(_BF16), seq, (w_out[0], ffn2_w_gate[0], ffn2_w_up[0], ffn2_w_down[0]))
    out = _mixer_ffn_call(
        attn_sinks[0], x1, z, conv_w[0], wout, ffn2_norm, wg2, wu2, wd2,
        final_norm.reshape(1, D_MODEL), seq)
    return out.reshape(batch, seq, D_MODEL)
```

```python
import functools

import jax
import jax.numpy as jnp
from jax import lax
from jax.experimental import pallas as pl
from jax.experimental.pallas import tpu as pltpu

D_MODEL = 1024
D_FF = 2816
CONV_WIDTH = 512
CONV_KERNEL = 3
N_Q_HEADS = 8
N_KV_HEADS = 2
Q_PER_KV = N_Q_HEADS // N_KV_HEADS
HEAD_DIM = 64
ATTN_WIDTH = N_Q_HEADS * HEAD_DIM
KV_WIDTH = N_KV_HEADS * HEAD_DIM
WINDOW = 128
BLOCK = 128
ROPE_THETA = 500000.0
ROT_DIM = HEAD_DIM // 4
ROT_HALF = ROT_DIM // 2
MIX_WIDTH = CONV_WIDTH + ATTN_WIDTH
IN_PROJ_WIDTH = 3 * CONV_WIDTH + ATTN_WIDTH + 2 * KV_WIDTH
Z_BG, Z_CU, Z_Q, Z_KV = 0, CONV_WIDTH, 2 * CONV_WIDTH, 2 * CONV_WIDTH + ATTN_WIDTH
Z_WIDTH = Z_KV + 2 * KV_WIDTH
FFN_RES_SCALE = 0.5
RMS_EPS = 1e-5
MASK_VALUE = -1e30

LANES = 128
BF16_SUBLANES = 16
FFN_TILE = 512
MIX_TILE = 512
FFN1_CHUNK = 512
FFN2_CHUNK = 512
SCORE_LOOKAHEAD = 2
VMEM_LIMIT_BYTES = 56 * 1024 * 1024

_F32 = jnp.float32
_BF16 = jnp.bfloat16


def _rms_norm(x, gain):
    inv = lax.rsqrt(jnp.mean(x * x, axis=-1, keepdims=True) + RMS_EPS)
    return (x * inv) * gain


def _ffn_dots(h, wg_ref, wu_ref, wd_ref, chunk, defer_down, result):
    acc = None

    def down(g, u, c0, c1):
        a = (g * jax.nn.sigmoid(g) * u).astype(_BF16)
        part = jnp.dot(a, wd_ref[c0:c1, :], preferred_element_type=_F32)
        result["acc"] = part if acc is None else acc + part
        return result["acc"]

    pending = None
    for c0 in range(0, D_FF, chunk):
        c1 = min(c0 + chunk, D_FF)
        g = jnp.dot(h, wg_ref[:, c0:c1], preferred_element_type=_F32)
        yield
        u = jnp.dot(h, wu_ref[:, c0:c1], preferred_element_type=_F32)
        yield
        if pending is not None:
            acc = down(*pending)
            yield
        pending = (g, u, c0, c1)
        if not defer_down:
            acc = down(*pending)
            pending = None
            yield
    if pending is not None:
        acc = down(*pending)
        yield


def _swiglu_residual(x, gain, wg_ref, wu_ref, wd_ref):
    result = {}
    for _ in _ffn_dots(_rms_norm(x, gain).astype(_BF16), wg_ref, wu_ref, wd_ref, FFN1_CHUNK, True, result):
        pass
    return x + FFN_RES_SCALE * result["acc"]


def _rotate_head_pair(t, cos, sin_lo, sin_hi):
    return t * cos + pltpu.roll(t, LANES - ROT_HALF, 1) * sin_lo + pltpu.roll(t, ROT_HALF, 1) * sin_hi


def _rotary_coefficients(tile_cs, row_tab):
    ca, sa = tile_cs[:, :LANES], tile_cs[:, LANES:]
    tab = [row_tab[:, j * LANES:(j + 1) * LANES] for j in range(6)]
    return ca * tab[0] - sa * tab[1], sa * tab[2] + ca * tab[3], sa * tab[4] + ca * tab[5]


def _mixer_inputs(z_ref):
    return (z_ref.at[:, pl.ds(Z_BG, CONV_WIDTH)], z_ref.at[:, pl.ds(Z_CU, CONV_WIDTH)],
            z_ref.at[:, pl.ds(Z_Q, ATTN_WIDTH)], z_ref.at[:, pl.ds(Z_KV, 2 * KV_WIDTH)])


def _inproj_pieces(h, win_ref, tile_cs, row_tab, bg_ref, cu_ref, q_ref, kv_ref):
    def project(c0, width):
        return jnp.dot(h, win_ref[:, c0:c0 + width], preferred_element_type=_F32)

    cos, sin_lo, sin_hi = _rotary_coefficients(tile_cs, row_tab)
    q = project(3 * CONV_WIDTH, ATTN_WIDTH)
    for j in range(ATTN_WIDTH // LANES):
        q_ref[:, j * LANES:(j + 1) * LANES] = (
            _rotate_head_pair(q[:, j * LANES:(j + 1) * LANES], cos, sin_lo, sin_hi) * (HEAD_DIM ** -0.5)
        ).astype(_BF16)
    yield
    kv = project(3 * CONV_WIDTH + ATTN_WIDTH, 2 * KV_WIDTH)
    kv_ref[:, :KV_WIDTH] = _rotate_head_pair(kv[:, :KV_WIDTH], cos, sin_lo, sin_hi).astype(_BF16)
    kv_ref[:, KV_WIDTH:] = kv[:, KV_WIDTH:].astype(_BF16)
    yield
    bg_ref[...] = project(0, CONV_WIDTH).astype(_BF16)
    yield
    c_gate = project(CONV_WIDTH, CONV_WIDTH)
    yield
    cu_ref[...] = (c_gate * project(2 * CONV_WIDTH, CONV_WIDTH)).astype(_BF16)
    yield


def _ffn_inproj_kernel(x_ref, g1_ref, wg_ref, wu_ref, wd_ref, g2_ref, win_ref, tile_cs_ref, row_tab_ref,
                       wout_f32_ref, wg2_f32_ref, wu2_f32_ref, wd2_f32_ref,
                       x1_ref, z_ref, wout_ref, wg2_ref, wu2_ref, wd2_ref, h_ref, *, n_tiles):
    for src_ref, dst_ref in ((wout_f32_ref, wout_ref), (wg2_f32_ref, wg2_ref), (wu2_f32_ref, wu2_ref),
                             (wd2_f32_ref, wd2_ref)):
        dst_ref[...] = src_ref[...].astype(_BF16)
    step = pl.program_id(0)

    def run(with_ffn):
        inproj = _inproj_pieces(h_ref[(step + 1) % 2], win_ref, tile_cs_ref[0], row_tab_ref[...],
                                *_mixer_inputs(z_ref))
        next(inproj)
        if with_ffn:
            x1 = _swiglu_residual(x_ref[...], g1_ref[...], wg_ref, wu_ref, wd_ref)
            x1_ref[...] = x1
            h_ref[step % 2] = _rms_norm(x1, g2_ref[...]).astype(_BF16)
        for _ in inproj:
            pass

    @pl.when(step == 0)
    def _():
        h_ref[1] = jnp.zeros(h_ref.shape[1:], h_ref.dtype)

    pl.when(step < n_tiles)(functools.partial(run, True))
    pl.when(step == n_tiles)(functools.partial(run, False))


def _conv_mixer(first_tile, bg_ref, cu_ref, cu_halo_ref, convw_ref):
    rows = cu_ref.shape[0]
    v = cu_ref[...].astype(_F32)
    halo = jnp.where(first_tile, 0.0, cu_halo_ref[...].astype(_F32))
    prev1 = halo[BF16_SUBLANES - 1:BF16_SUBLANES, :]
    prev2 = halo[BF16_SUBLANES - 2:BF16_SUBLANES - 1, :]
    row = lax.broadcasted_iota(jnp.int32, (rows, 1), 0)
    v1 = jnp.where(row == 0, prev1, pltpu.roll(v, 1, 0))
    v2 = jnp.where(row == 0, prev2, jnp.where(row == 1, prev1, pltpu.roll(v, 2, 0)))
    w = convw_ref[...]
    return bg_ref[...].astype(_F32) * (w[0:1, :] * v2 + w[1:2, :] * v1 + w[2:3, :] * v)


def _attention_pieces(first_tile, sinks_ref, q_ref, kv_ref, kv_halo_ref, result):
    rows = q_ref.shape[0]
    kv_ext = jnp.concatenate([kv_halo_ref[...], kv_ref[...]], axis=0)
    stacked = Q_PER_KV * BLOCK
    row = lax.broadcasted_iota(jnp.int32, (stacked, 2 * BLOCK), 0)
    qi = row % BLOCK
    kj = lax.broadcasted_iota(jnp.int32, (stacked, 2 * BLOCK), 1)
    in_window = (kj > qi) & (kj <= qi + BLOCK)
    first_mask = in_window & ((kj >= BLOCK) | jnp.logical_not(first_tile))
    group = lax.broadcasted_iota(jnp.int32, (stacked, 1), 0) // BLOCK

    def sink_column(hkv):
        col = jnp.full((stacked, 1), sinks_ref[hkv * Q_PER_KV], _F32)
        for g in range(1, Q_PER_KV):
            col = jnp.where(group == g, sinks_ref[hkv * Q_PER_KV + g], col)
        return col

    sink_columns = [sink_column(hkv) for hkv in range(N_KV_HEADS)]

    def scores(blk, hkv):
        r0 = blk * BLOCK
        k = kv_ext[r0:r0 + 2 * BLOCK, hkv * HEAD_DIM:(hkv + 1) * HEAD_DIM]
        q = jnp.concatenate([q_ref[r0:r0 + BLOCK, hq * HEAD_DIM:(hq + 1) * HEAD_DIM]
                             for hq in range(hkv * Q_PER_KV, (hkv + 1) * Q_PER_KV)], axis=0)
        return lax.dot_general(q, k, (((1,), (1,)), ((), ())), preferred_element_type=_F32)

    def weighted_values(blk, hkv, s):
        r0 = blk * BLOCK
        vv = kv_ext[r0:r0 + 2 * BLOCK, KV_WIDTH + hkv * HEAD_DIM:KV_WIDTH + (hkv + 1) * HEAD_DIM]
        sink = sink_columns[hkv]
        s = jnp.where(in_window if blk > 0 else first_mask, s, MASK_VALUE)
        m = jnp.maximum(jnp.max(s, axis=-1, keepdims=True), sink)
        p = jnp.exp(s - m)
        denom = jnp.sum(p, axis=-1, keepdims=True) + jnp.exp(sink - m)
        o = jnp.dot(p.astype(_BF16), vv, preferred_element_type=_F32) / denom
        return [o[g * BLOCK:(g + 1) * BLOCK] for g in range(Q_PER_KV)]

    pieces = [(blk, hkv) for blk in range(rows // BLOCK) for hkv in range(N_KV_HEADS)]
    pending = [scores(*piece) for piece in pieces[:SCORE_LOOKAHEAD]]
    yield
    blocks, heads = [], []
    for i, piece in enumerate(pieces):
        heads.extend(weighted_values(*piece, pending.pop(0)))
        if len(heads) == N_Q_HEADS:
            blocks.append(jnp.concatenate(heads, axis=1))
            heads = []
        if i + SCORE_LOOKAHEAD < len(pieces):
            pending.append(scores(*pieces[i + SCORE_LOOKAHEAD]))
        if i + 1 < len(pieces):
            yield
    result["y_attn"] = jnp.concatenate(blocks, axis=0)


def _mixer_ffn_kernel(sinks_ref, x1_ref, z_ref, cu_halo_ref, kv_halo_ref, convw_ref, wout_ref,
                      g_ref, wg_ref, wu_ref, wd_ref, gf_ref, out_ref, x2_ref, h_ref, *, tiles_per_seq):
    step = pl.program_id(0)

    @pl.when(step == 0)
    def _():
        x2_ref[1] = jnp.zeros(x2_ref.shape[1:], x2_ref.dtype)
        h_ref[1] = jnp.zeros(h_ref.shape[1:], h_ref.dtype)

    bg_ref, cu_ref, q_ref, kv_ref = _mixer_inputs(z_ref)
    read_slot = (step + 1) % 2
    write_slot = step % 2
    first_tile = step % tiles_per_seq == 0
    ffn, att = {}, {}
    ffn_dots = _ffn_dots(h_ref[read_slot], wg_ref, wu_ref, wd_ref, FFN2_CHUNK, False, ffn)
    next(ffn_dots)
    y_conv = _conv_mixer(first_tile, bg_ref, cu_ref, cu_halo_ref, convw_ref)
    for _ in _attention_pieces(first_tile, sinks_ref, q_ref, kv_ref, kv_halo_ref, att):
        next(ffn_dots, None)
    y = jnp.concatenate([y_conv, att["y_attn"]], axis=1).astype(_BF16)
    x2 = x1_ref[...] + jnp.dot(y, wout_ref[...], preferred_element_type=_F32)
    x2_ref[write_slot] = x2
    h_ref[write_slot] = _rms_norm(x2, g_ref[...]).astype(_BF16)
    for _ in ffn_dots:
        pass
    x = x2_ref[read_slot] + FFN_RES_SCALE * ffn["acc"]
    out_ref[...] = _rms_norm(x, gf_ref[...])


def _resident(shape):
    return pl.BlockSpec(shape, lambda *_: (0,) * len(shape), pipeline_mode=pl.Buffered(1))


def _rotary_tables(seq, tile):
    inv_freq = ROPE_THETA ** (-jnp.arange(0, ROT_DIM, 2, dtype=_F32) / ROT_DIM)
    dim = jnp.arange(LANES) % HEAD_DIM
    freq = jnp.where(dim < ROT_DIM, inv_freq[dim % ROT_HALF], 0.0)
    lo = (dim < ROT_HALF).astype(_F32)
    hi = ((dim >= ROT_HALF) & (dim < ROT_DIM)).astype(_F32)
    start = (jnp.arange(seq // tile) * tile).astype(_F32)[:, None] * freq[None, :]
    row = jnp.arange(tile, dtype=_F32)[:, None] * freq[None, :]
    tile_cs = jnp.concatenate([jnp.cos(start), jnp.sin(start)], axis=1).reshape(seq // tile, 1, 2 * LANES)
    cb, sb = jnp.cos(row), jnp.sin(row)
    return tile_cs, jnp.concatenate([cb, sb, -cb * lo, -sb * lo, cb * hi, sb * hi], axis=1)


_FFN_WEIGHT_SPECS = [_resident((D_MODEL, D_FF)), _resident((D_MODEL, D_FF)), _resident((D_FF, D_MODEL))]
_PARAMS = functools.partial(pltpu.CompilerParams, vmem_limit_bytes=VMEM_LIMIT_BYTES)


def _cast_block_rows(rows, steps):
    return next(r for r in range(BF16_SUBLANES, rows + 1, BF16_SUBLANES) if rows % r == 0 and r * steps >= rows)


def _ffn_inproj_call(xt, gain1, wg, wu, wd, gain2, win, seq, later_weights):
    tokens = xt.shape[0]
    n_tiles = tokens // FFN_TILE
    seq_tiles = seq // FFN_TILE
    cast_specs, cast_shapes = [], []
    for w in later_weights:
        rows, cols = w.shape
        block_rows = _cast_block_rows(rows, n_tiles)
        index_map = functools.partial(lambda s, last: (jnp.minimum(s, last), 0), last=rows // block_rows - 1)
        cast_specs.append(pl.BlockSpec((block_rows, cols), index_map))
        cast_shapes.append(jax.ShapeDtypeStruct(w.shape, _BF16))

    def ffn_spec(width):
        return pl.BlockSpec((FFN_TILE, width), lambda s: (jnp.minimum(s, n_tiles - 1), 0))

    def proj_spec(width):
        return pl.BlockSpec((FFN_TILE, width), lambda s: (jnp.maximum(s - 1, 0), 0))

    return pl.pallas_call(
        functools.partial(_ffn_inproj_kernel, n_tiles=n_tiles),
        grid=(n_tiles + 1,),
        in_specs=[ffn_spec(D_MODEL), _resident((1, D_MODEL)), *_FFN_WEIGHT_SPECS,
                  _resident((1, D_MODEL)), _resident((D_MODEL, IN_PROJ_WIDTH)),
                  pl.BlockSpec((1, 1, 2 * LANES), lambda s: (jnp.maximum(s - 1, 0) % seq_tiles, 0, 0)),
                  _resident((FFN_TILE, 6 * LANES)), *cast_specs],
        out_specs=[ffn_spec(D_MODEL), proj_spec(Z_WIDTH), *cast_specs],
        out_shape=[jax.ShapeDtypeStruct((tokens, D_MODEL), _F32),
                   jax.ShapeDtypeStruct((tokens, Z_WIDTH), _BF16), *cast_shapes],
        scratch_shapes=[pltpu.VMEM((2, FFN_TILE, D_MODEL), _BF16)],
        compiler_params=_PARAMS(dimension_semantics=("arbitrary",)),
        name="ffn1_inproj",
    )(xt, gain1, wg, wu, wd, gain2, win, *_rotary_tables(seq, FFN_TILE), *later_weights)


def _mixer_ffn_call(sinks, x1, z, conv_w, wout, gain, wg, wu, wd, final_gain, seq):
    tokens = x1.shape[0]
    n_tiles = tokens // MIX_TILE
    halo_per_tile = MIX_TILE // BF16_SUBLANES
    blocks_per_tile = MIX_TILE // BLOCK

    def mix_tile(s):
        return jnp.minimum(s, n_tiles - 1)

    def tile_spec(width):
        return pl.BlockSpec((MIX_TILE, width), lambda s, sinks: (mix_tile(s), 0))

    return pl.pallas_call(
        functools.partial(_mixer_ffn_kernel, tiles_per_seq=seq // MIX_TILE),
        grid_spec=pltpu.PrefetchScalarGridSpec(
            num_scalar_prefetch=1,
            grid=(n_tiles + 1,),
            in_specs=[
                tile_spec(D_MODEL), tile_spec(Z_WIDTH),
                pl.BlockSpec((BF16_SUBLANES, CONV_WIDTH),
                             lambda s, sinks: (jnp.maximum(mix_tile(s) * halo_per_tile - 1, 0), Z_CU // CONV_WIDTH)),
                pl.BlockSpec((BLOCK, 2 * KV_WIDTH),
                             lambda s, sinks: (jnp.maximum(mix_tile(s) * blocks_per_tile - 1, 0),
                                               Z_KV // (2 * KV_WIDTH))),
                _resident((CONV_KERNEL, CONV_WIDTH)), _resident((MIX_WIDTH, D_MODEL)),
                _resident((1, D_MODEL)), *_FFN_WEIGHT_SPECS, _resident((1, D_MODEL)),
            ],
            out_specs=pl.BlockSpec((MIX_TILE, D_MODEL), lambda s, sinks: (jnp.maximum(s - 1, 0), 0)),
            scratch_shapes=[pltpu.VMEM((2, MIX_TILE, D_MODEL), _F32), pltpu.VMEM((2, MIX_TILE, D_MODEL), _BF16)],
        ),
        out_shape=jax.ShapeDtypeStruct((tokens, D_MODEL), _F32),
        compiler_params=_PARAMS(dimension_semantics=("arbitrary",)),
        name="mixer_ffn2",
    )(sinks, x1, z, z, z, conv_w, wout, gain, wg, wu, wd, final_gain)


def kernel(x, ffn1_norm, ffn1_w_gate, ffn1_w_up, ffn1_w_down, mix_norm, w_in, conv_w, attn_sinks, w_out,
           ffn2_norm, ffn2_w_gate, ffn2_w_up, ffn2_w_down, final_norm):
    batch, seq, d_model = x.shape
    assert d_model == D_MODEL and ffn1_norm.shape[0] == 1
    assert seq % MIX_TILE == 0 and seq % FFN_TILE == 0
    xt = x.reshape(batch * seq, D_MODEL)
    x1, z, wout, wg2, wu2, wd2 = _ffn_inproj_call(
        xt, ffn1_norm, ffn1_w_gate[0].astype(_BF16), ffn1_w_up[0].astype(_BF16), ffn1_w_down[0].astype(_BF16),
        mix_norm, w_in[0].astype(_BF16), seq, (w_out[0], ffn2_w_gate[0], ffn2_w_up[0], ffn2_w_down[0]))
    out = _mixer_ffn_call(
        attn_sinks[0], x1, z, conv_w[0], wout, ffn2_norm, wg2, wu2, wd2,
        final_norm.reshape(1, D_MODEL), seq)
    return out.reshape(batch, seq, D_MODEL)
```

```python
import functools

import jax
import jax.numpy as jnp
from jax import lax
from jax.experimental import pallas as pl
from jax.experimental.pallas import tpu as pltpu

D_MODEL = 1024
D_FF = 2816
CONV_WIDTH = 512
CONV_KERNEL = 3
N_Q_HEADS = 8
N_KV_HEADS = 2
Q_PER_KV = N_Q_HEADS // N_KV_HEADS
HEAD_DIM = 64
ATTN_WIDTH = N_Q_HEADS * HEAD_DIM
KV_WIDTH = N_KV_HEADS * HEAD_DIM
WINDOW = 128
BLOCK = 128
ROPE_THETA = 500000.0
ROT_DIM = HEAD_DIM // 4
ROT_HALF = ROT_DIM // 2
MIX_WIDTH = CONV_WIDTH + ATTN_WIDTH
IN_PROJ_WIDTH = 3 * CONV_WIDTH + ATTN_WIDTH + 2 * KV_WIDTH
Z_BG, Z_CU, Z_Q, Z_KV = 0, CONV_WIDTH, 2 * CONV_WIDTH, 2 * CONV_WIDTH + ATTN_WIDTH
Z_WIDTH = Z_KV + 2 * KV_WIDTH
FFN_RES_SCALE = 0.5
RMS_EPS = 1e-5
MASK_VALUE = -1e30

LANES = 128
BF16_SUBLANES = 16
FFN_TILE = 512
MIX_TILE = 512
FFN1_CHUNK = 512
FFN2_CHUNK = 512
SCORE_LOOKAHEAD = 2
CAST_CHUNK_BYTES = 720 * 1024
CAST_SLOTS = 4
VMEM_LIMIT_BYTES = 56 * 1024 * 1024

_F32 = jnp.float32
_BF16 = jnp.bfloat16


def _rms_norm(x, gain):
    inv = lax.rsqrt(jnp.mean(x * x, axis=-1, keepdims=True) + RMS_EPS)
    return (x * inv) * gain


def _ffn_dots(h, wg_ref, wu_ref, wd_ref, chunk, defer_down, result):
    acc = None

    def down(g, u, c0, c1):
        a = (g * jax.nn.sigmoid(g) * u).astype(_BF16)
        part = jnp.dot(a, wd_ref[c0:c1, :], preferred_element_type=_F32)
        result["acc"] = part if acc is None else acc + part
        return result["acc"]

    pending = None
    for c0 in range(0, D_FF, chunk):
        c1 = min(c0 + chunk, D_FF)
        g = jnp.dot(h, wg_ref[:, c0:c1], preferred_element_type=_F32)
        yield
        u = jnp.dot(h, wu_ref[:, c0:c1], preferred_element_type=_F32)
        yield
        if pending is not None:
            acc = down(*pending)
            yield
        pending = (g, u, c0, c1)
        if not defer_down:
            acc = down(*pending)
            pending = None
            yield
    if pending is not None:
        acc = down(*pending)
        yield


def _swiglu_residual(x, gain, wg_ref, wu_ref, wd_ref):
    result = {}
    for _ in _ffn_dots(_rms_norm(x, gain).astype(_BF16), wg_ref, wu_ref, wd_ref, FFN1_CHUNK, True, result):
        pass
    return x + FFN_RES_SCALE * result["acc"]


def _rotate_head_pair(t, cos, sin_lo, sin_hi):
    return t * cos + pltpu.roll(t, LANES - ROT_HALF, 1) * sin_lo + pltpu.roll(t, ROT_HALF, 1) * sin_hi


def _rotary_coefficients(tile_cs, row_tab):
    ca, sa = tile_cs[:, :LANES], tile_cs[:, LANES:]
    tab = [row_tab[:, j * LANES:(j + 1) * LANES] for j in range(6)]
    return ca * tab[0] - sa * tab[1], sa * tab[2] + ca * tab[3], sa * tab[4] + ca * tab[5]


def _mixer_inputs(z_ref):
    return (z_ref.at[:, pl.ds(Z_BG, CONV_WIDTH)], z_ref.at[:, pl.ds(Z_CU, CONV_WIDTH)],
            z_ref.at[:, pl.ds(Z_Q, ATTN_WIDTH)], z_ref.at[:, pl.ds(Z_KV, 2 * KV_WIDTH)])


def _inproj_pieces(h, win_ref, tile_cs, row_tab, bg_ref, cu_ref, q_ref, kv_ref):
    def project(c0, width):
        return jnp.dot(h, win_ref[:, c0:c0 + width], preferred_element_type=_F32)

    cos, sin_lo, sin_hi = _rotary_coefficients(tile_cs, row_tab)
    q = project(3 * CONV_WIDTH, ATTN_WIDTH)
    for j in range(ATTN_WIDTH // LANES):
        q_ref[:, j * LANES:(j + 1) * LANES] = (
            _rotate_head_pair(q[:, j * LANES:(j + 1) * LANES], cos, sin_lo, sin_hi) * (HEAD_DIM ** -0.5)
        ).astype(_BF16)
    yield
    kv = project(3 * CONV_WIDTH + ATTN_WIDTH, 2 * KV_WIDTH)
    kv_ref[:, :KV_WIDTH] = _rotate_head_pair(kv[:, :KV_WIDTH], cos, sin_lo, sin_hi).astype(_BF16)
    kv_ref[:, KV_WIDTH:] = kv[:, KV_WIDTH:].astype(_BF16)
    yield
    bg_ref[...] = project(0, CONV_WIDTH).astype(_BF16)
    yield
    c_gate = project(CONV_WIDTH, CONV_WIDTH)
    yield
    cu_ref[...] = (c_gate * project(2 * CONV_WIDTH, CONV_WIDTH)).astype(_BF16)
    yield


def _cast_chunk_rows(rows, cols):
    fits = [r for r in range(BF16_SUBLANES, rows + 1, BF16_SUBLANES) if rows % r == 0 and r * cols * 4 <= CAST_CHUNK_BYTES]
    return max(fits)


def _cast_to_vmem(src_hbm_ref, dst_ref, stage_ref, sem_ref):
    n_slots, chunk = stage_ref.shape[:2]
    n_chunks = src_hbm_ref.shape[0] // chunk
    ahead = n_slots - 1

    def copy(k):
        slot = k % n_slots
        return pltpu.make_async_copy(src_hbm_ref.at[pl.ds(k * chunk, chunk)], stage_ref.at[slot], sem_ref.at[slot])

    for k in range(min(ahead, n_chunks)):
        copy(k).start()

    def body(k, carry):
        @pl.when(k + ahead < n_chunks)
        def _():
            copy(k + ahead).start()

        copy(k).wait()
        dst_ref[pl.ds(pl.multiple_of(k * chunk, chunk), chunk), :] = stage_ref[k % n_slots].astype(_BF16)
        return carry

    lax.fori_loop(0, n_chunks, body, 0)


def _ffn_inproj_kernel(x_ref, g1_ref, wg_hbm_ref, wu_hbm_ref, wd_hbm_ref, g2_ref, win_hbm_ref, tile_cs_ref, row_tab_ref,
                       wout_f32_ref, wg2_f32_ref, wu2_f32_ref, wd2_f32_ref,
                       x1_ref, z_ref, wout_ref, wg2_ref, wu2_ref, wd2_ref,
                       h_ref, wg_ref, wu_ref, wd_ref, win_ref, stage_ff_ref, stage_dn_ref, stage_in_ref, sem_ref,
                       *, n_tiles):
    for src_ref, dst_ref in ((wout_f32_ref, wout_ref), (wg2_f32_ref, wg2_ref), (wu2_f32_ref, wu2_ref),
                             (wd2_f32_ref, wd2_ref)):
        dst_ref[...] = src_ref[...].astype(_BF16)
    step = pl.program_id(0)

    @pl.when(step == 0)
    def _():
        _cast_to_vmem(wg_hbm_ref, wg_ref, stage_ff_ref, sem_ref)
        _cast_to_vmem(wu_hbm_ref, wu_ref, stage_ff_ref, sem_ref)
        _cast_to_vmem(wd_hbm_ref, wd_ref, stage_dn_ref, sem_ref)
        _cast_to_vmem(win_hbm_ref, win_ref, stage_in_ref, sem_ref)

    def run(with_ffn):
        inproj = _inproj_pieces(h_ref[(step + 1) % 2], win_ref, tile_cs_ref[0], row_tab_ref[...],
                                *_mixer_inputs(z_ref))
        next(inproj)
        if with_ffn:
            x1 = _swiglu_residual(x_ref[...], g1_ref[...], wg_ref, wu_ref, wd_ref)
            x1_ref[...] = x1
            h_ref[step % 2] = _rms_norm(x1, g2_ref[...]).astype(_BF16)
        for _ in inproj:
            pass

    @pl.when(step == 0)
    def _():
        h_ref[1] = jnp.zeros(h_ref.shape[1:], h_ref.dtype)

    pl.when(step < n_tiles)(functools.partial(run, True))
    pl.when(step == n_tiles)(functools.partial(run, False))


def _conv_mixer(first_tile, bg_ref, cu_ref, cu_halo_ref, convw_ref):
    rows = cu_ref.shape[0]
    v = cu_ref[...].astype(_F32)
    halo = jnp.where(first_tile, 0.0, cu_halo_ref[...].astype(_F32))
    prev1 = halo[BF16_SUBLANES - 1:BF16_SUBLANES, :]
    prev2 = halo[BF16_SUBLANES - 2:BF16_SUBLANES - 1, :]
    row = lax.broadcasted_iota(jnp.int32, (rows, 1), 0)
    v1 = jnp.where(row == 0, prev1, pltpu.roll(v, 1, 0))
    v2 = jnp.where(row == 0, prev2, jnp.where(row == 1, prev1, pltpu.roll(v, 2, 0)))
    w = convw_ref[...]
    return bg_ref[...].astype(_F32) * (w[0:1, :] * v2 + w[1:2, :] * v1 + w[2:3, :] * v)


def _attention_pieces(first_tile, sinks_ref, q_ref, kv_ref, kv_halo_ref, result):
    rows = q_ref.shape[0]
    kv_ext = jnp.concatenate([kv_halo_ref[...], kv_ref[...]], axis=0)
    stacked = Q_PER_KV * BLOCK
    row = lax.broadcasted_iota(jnp.int32, (stacked, 2 * BLOCK), 0)
    qi = row % BLOCK
    kj = lax.broadcasted_iota(jnp.int32, (stacked, 2 * BLOCK), 1)
    in_window = (kj > qi) & (kj <= qi + BLOCK)
    first_mask = in_window & ((kj >= BLOCK) | jnp.logical_not(first_tile))
    group = lax.broadcasted_iota(jnp.int32, (stacked, 1), 0) // BLOCK

    def sink_column(hkv):
        col = jnp.full((stacked, 1), sinks_ref[hkv * Q_PER_KV], _F32)
        for g in range(1, Q_PER_KV):
            col = jnp.where(group == g, sinks_ref[hkv * Q_PER_KV + g], col)
        return col

    sink_columns = [sink_column(hkv) for hkv in range(N_KV_HEADS)]

    def scores(blk, hkv):
        r0 = blk * BLOCK
        k = kv_ext[r0:r0 + 2 * BLOCK, hkv * HEAD_DIM:(hkv + 1) * HEAD_DIM]
        q = jnp.concatenate([q_ref[r0:r0 + BLOCK, hq * HEAD_DIM:(hq + 1) * HEAD_DIM]
                             for hq in range(hkv * Q_PER_KV, (hkv + 1) * Q_PER_KV)], axis=0)
        return lax.dot_general(q, k, (((1,), (1,)), ((), ())), preferred_element_type=_F32)

    def weighted_values(blk, hkv, s):
        r0 = blk * BLOCK
        vv = kv_ext[r0:r0 + 2 * BLOCK, KV_WIDTH + hkv * HEAD_DIM:KV_WIDTH + (hkv + 1) * HEAD_DIM]
        sink = sink_columns[hkv]
        s = jnp.where(in_window if blk > 0 else first_mask, s, MASK_VALUE)
        m = jnp.maximum(jnp.max(s, axis=-1, keepdims=True), sink)
        p = jnp.exp(s - m)
        denom = jnp.sum(p, axis=-1, keepdims=True) + jnp.exp(sink - m)
        o = jnp.dot(p.astype(_BF16), vv, preferred_element_type=_F32) / denom
        return [o[g * BLOCK:(g + 1) * BLOCK] for g in range(Q_PER_KV)]

    pieces = [(blk, hkv) for blk in range(rows // BLOCK) for hkv in range(N_KV_HEADS)]
    pending = [scores(*piece) for piece in pieces[:SCORE_LOOKAHEAD]]
    yield
    blocks, heads = [], []
    for i, piece in enumerate(pieces):
        heads.extend(weighted_values(*piece, pending.pop(0)))
        if len(heads) == N_Q_HEADS:
            blocks.append(jnp.concatenate(heads, axis=1))
            heads = []
        if i + SCORE_LOOKAHEAD < len(pieces):
            pending.append(scores(*pieces[i + SCORE_LOOKAHEAD]))
        if i + 1 < len(pieces):
            yield
    result["y_attn"] = jnp.concatenate(blocks, axis=0)


def _mixer_ffn_kernel(sinks_ref, x1_ref, z_ref, cu_halo_ref, kv_halo_ref, convw_ref, wout_ref,
                      g_ref, wg_ref, wu_ref, wd_ref, gf_ref, out_ref, x2_ref, h_ref, *, tiles_per_seq):
    step = pl.program_id(0)

    @pl.when(step == 0)
    def _():
        x2_ref[1] = jnp.zeros(x2_ref.shape[1:], x2_ref.dtype)
        h_ref[1] = jnp.zeros(h_ref.shape[1:], h_ref.dtype)

    bg_ref, cu_ref, q_ref, kv_ref = _mixer_inputs(z_ref)
    read_slot = (step + 1) % 2
    write_slot = step % 2
    first_tile = step % tiles_per_seq == 0
    ffn, att = {}, {}
    ffn_dots = _ffn_dots(h_ref[read_slot], wg_ref, wu_ref, wd_ref, FFN2_CHUNK, False, ffn)
    next(ffn_dots)
    y_conv = _conv_mixer(first_tile, bg_ref, cu_ref, cu_halo_ref, convw_ref)
    for _ in _attention_pieces(first_tile, sinks_ref, q_ref, kv_ref, kv_halo_ref, att):
        next(ffn_dots, None)
    y = jnp.concatenate([y_conv, att["y_attn"]], axis=1).astype(_BF16)
    x2 = x1_ref[...] + jnp.dot(y, wout_ref[...], preferred_element_type=_F32)
    x2_ref[write_slot] = x2
    h_ref[write_slot] = _rms_norm(x2, g_ref[...]).astype(_BF16)
    for _ in ffn_dots:
        pass
    x = x2_ref[read_slot] + FFN_RES_SCALE * ffn["acc"]
    out_ref[...] = _rms_norm(x, gf_ref[...])


def _resident(shape):
    return pl.BlockSpec(shape, lambda *_: (0,) * len(shape), pipeline_mode=pl.Buffered(1))


def _rotary_tables(seq, tile):
    inv_freq = ROPE_THETA ** (-jnp.arange(0, ROT_DIM, 2, dtype=_F32) / ROT_DIM)
    dim = jnp.arange(LANES) % HEAD_DIM
    freq = jnp.where(dim < ROT_DIM, inv_freq[dim % ROT_HALF], 0.0)
    lo = (dim < ROT_HALF).astype(_F32)
    hi = ((dim >= ROT_HALF) & (dim < ROT_DIM)).astype(_F32)
    start = (jnp.arange(seq // tile) * tile).astype(_F32)[:, None] * freq[None, :]
    row = jnp.arange(tile, dtype=_F32)[:, None] * freq[None, :]
    tile_cs = jnp.concatenate([jnp.cos(start), jnp.sin(start)], axis=1).reshape(seq // tile, 1, 2 * LANES)
    cb, sb = jnp.cos(row), jnp.sin(row)
    return tile_cs, jnp.concatenate([cb, sb, -cb * lo, -sb * lo, cb * hi, sb * hi], axis=1)


_FFN_WEIGHT_SPECS = [_resident((D_MODEL, D_FF)), _resident((D_MODEL, D_FF)), _resident((D_FF, D_MODEL))]
_PARAMS = functools.partial(pltpu.CompilerParams, vmem_limit_bytes=VMEM_LIMIT_BYTES)


def _cast_block_rows(rows, steps):
    return next(r for r in range(BF16_SUBLANES, rows + 1, BF16_SUBLANES) if rows % r == 0 and r * steps >= rows)


def _ffn_inproj_call(xt, gain1, wg, wu, wd, gain2, win, seq, later_weights):
    tokens = xt.shape[0]
    n_tiles = tokens // FFN_TILE
    seq_tiles = seq // FFN_TILE
    hbm = pl.BlockSpec(memory_space=pl.ANY)
    cast_specs, cast_shapes = [], []
    for w in later_weights:
        rows, cols = w.shape
        block_rows = _cast_block_rows(rows, n_tiles)
        index_map = functools.partial(lambda s, last: (jnp.minimum(s, last), 0), last=rows // block_rows - 1)
        cast_specs.append(pl.BlockSpec((block_rows, cols), index_map))
        cast_shapes.append(jax.ShapeDtypeStruct(w.shape, _BF16))

    def ffn_spec(width):
        return pl.BlockSpec((FFN_TILE, width), lambda s: (jnp.minimum(s, n_tiles - 1), 0))

    def proj_spec(width):
        return pl.BlockSpec((FFN_TILE, width), lambda s: (jnp.maximum(s - 1, 0), 0))

    return pl.pallas_call(
        functools.partial(_ffn_inproj_kernel, n_tiles=n_tiles),
        grid=(n_tiles + 1,),
        in_specs=[ffn_spec(D_MODEL), _resident((1, D_MODEL)), hbm, hbm, hbm,
                  _resident((1, D_MODEL)), hbm,
                  pl.BlockSpec((1, 1, 2 * LANES), lambda s: (jnp.maximum(s - 1, 0) % seq_tiles, 0, 0)),
                  _resident((FFN_TILE, 6 * LANES)), *cast_specs],
        out_specs=[ffn_spec(D_MODEL), proj_spec(Z_WIDTH), *cast_specs],
        out_shape=[jax.ShapeDtypeStruct((tokens, D_MODEL), _F32),
                   jax.ShapeDtypeStruct((tokens, Z_WIDTH), _BF16), *cast_shapes],
        scratch_shapes=[pltpu.VMEM((2, FFN_TILE, D_MODEL), _BF16),
                        pltpu.VMEM(wg.shape, _BF16), pltpu.VMEM(wu.shape, _BF16), pltpu.VMEM(wd.shape, _BF16),
                        pltpu.VMEM(win.shape, _BF16),
                        *[pltpu.VMEM((CAST_SLOTS, _cast_chunk_rows(*w.shape), w.shape[1]), _F32)
                          for w in (wg, wd, win)],
                        pltpu.SemaphoreType.DMA((CAST_SLOTS,))],
        compiler_params=_PARAMS(dimension_semantics=("arbitrary",)),
        name="ffn1_inproj",
    )(xt, gain1, wg, wu, wd, gain2, win, *_rotary_tables(seq, FFN_TILE), *later_weights)


def _mixer_ffn_call(sinks, x1, z, conv_w, wout, gain, wg, wu, wd, final_gain, seq):
    tokens = x1.shape[0]
    n_tiles = tokens // MIX_TILE
    halo_per_tile = MIX_TILE // BF16_SUBLANES
    blocks_per_tile = MIX_TILE // BLOCK

    def mix_tile(s):
        return jnp.minimum(s, n_tiles - 1)

    def tile_spec(width):
        return pl.BlockSpec((MIX_TILE, width), lambda s, sinks: (mix_tile(s), 0))

    return pl.pallas_call(
        functools.partial(_mixer_ffn_kernel, tiles_per_seq=seq // MIX_TILE),
        grid_spec=pltpu.PrefetchScalarGridSpec(
            num_scalar_prefetch=1,
            grid=(n_tiles + 1,),
            in_specs=[
                tile_spec(D_MODEL), tile_spec(Z_WIDTH),
                pl.BlockSpec((BF16_SUBLANES, CONV_WIDTH),
                             lambda s, sinks: (jnp.maximum(mix_tile(s) * halo_per_tile - 1, 0), Z_CU // CONV_WIDTH)),
                pl.BlockSpec((BLOCK, 2 * KV_WIDTH),
                             lambda s, sinks: (jnp.maximum(mix_tile(s) * blocks_per_tile - 1, 0),
                                               Z_KV // (2 * KV_WIDTH))),
                _resident((CONV_KERNEL, CONV_WIDTH)), _resident((MIX_WIDTH, D_MODEL)),
                _resident((1, D_MODEL)), *_FFN_WEIGHT_SPECS, _resident((1, D_MODEL)),
            ],
            out_specs=pl.BlockSpec((MIX_TILE, D_MODEL), lambda s, sinks: (jnp.maximum(s - 1, 0), 0)),
            scratch_shapes=[pltpu.VMEM((2, MIX_TILE, D_MODEL), _F32), pltpu.VMEM((2, MIX_TILE, D_MODEL), _BF16)],
        ),
        out_shape=jax.ShapeDtypeStruct((tokens, D_MODEL), _F32),
        compiler_params=_PARAMS(dimension_semantics=("arbitrary",)),
        name="mixer_ffn2",
    )(sinks, x1, z, z, z, conv_w, wout, gain, wg, wu, wd, final_gain)


def kernel(x, ffn1_norm, ffn1_w_gate, ffn1_w_up, ffn1_w_down, mix_norm, w_in, conv_w, attn_sinks, w_out,
           ffn2_norm, ffn2_w_gate, ffn2_w_up, ffn2_w_down, final_norm):
    batch, seq, d_model = x.shape
    assert d_model == D_MODEL and ffn1_norm.shape[0] == 1
    assert seq % MIX_TILE == 0 and seq % FFN_TILE == 0
    xt = x.reshape(batch * seq, D_MODEL)
    x1, z, wout, wg2, wu2, wd2 = _ffn_inproj_call(
        xt, ffn1_norm, ffn1_w_gate[0], ffn1_w_up[0], ffn1_w_down[0], mix_norm, w_in[0], seq,
        (w_out[0], ffn2_w_gate[0], ffn2_w_up[0], ffn2_w_down[0]))
    out = _mixer_ffn_call(
        attn_sinks[0], x1, z, conv_w[0], wout, ffn2_norm, wg2, wu2, wd2,
        final_norm.reshape(1, D_MODEL), seq)
    return out.reshape(batch, seq, D_MODEL)
```

```python
import functools

import jax
import jax.numpy as jnp
from jax import lax
from jax.experimental import pallas as pl
from jax.experimental.pallas import tpu as pltpu

D_MODEL = 1024
D_FF = 2816
CONV_WIDTH = 512
CONV_KERNEL = 3
N_Q_HEADS = 8
N_KV_HEADS = 2
Q_PER_KV = N_Q_HEADS // N_KV_HEADS
HEAD_DIM = 64
ATTN_WIDTH = N_Q_HEADS * HEAD_DIM
KV_WIDTH = N_KV_HEADS * HEAD_DIM
WINDOW = 128
BLOCK = 128
ROPE_THETA = 500000.0
ROT_DIM = HEAD_DIM // 4
ROT_HALF = ROT_DIM // 2
MIX_WIDTH = CONV_WIDTH + ATTN_WIDTH
IN_PROJ_WIDTH = 3 * CONV_WIDTH + ATTN_WIDTH + 2 * KV_WIDTH
Z_BG, Z_CU, Z_Q, Z_KV = 0, CONV_WIDTH, 2 * CONV_WIDTH, 2 * CONV_WIDTH + ATTN_WIDTH
Z_WIDTH = Z_KV + 2 * KV_WIDTH
FFN_RES_SCALE = 0.5
RMS_EPS = 1e-5
MASK_VALUE = -1e30

LANES = 128
BF16_SUBLANES = 16
FFN_TILE = 512
MIX_TILE = 512
FFN1_CHUNK = 512
FFN2_CHUNK = 512
SCORE_LOOKAHEAD = 2
CAST_CHUNK_BYTES = 720 * 1024
CAST_SLOTS = 4
VMEM_LIMIT_BYTES = 56 * 1024 * 1024

_F32 = jnp.float32
_BF16 = jnp.bfloat16


def _rms_norm(x, gain):
    inv = lax.rsqrt(jnp.mean(x * x, axis=-1, keepdims=True) + RMS_EPS)
    return (x * inv) * gain


def _ffn_dots(h, wg_ref, wu_ref, wd_ref, chunk, defer_down, result):
    acc = None

    def down(g, u, c0, c1):
        a = (g * jax.nn.sigmoid(g) * u).astype(_BF16)
        part = jnp.dot(a, wd_ref[c0:c1, :], preferred_element_type=_F32)
        result["acc"] = part if acc is None else acc + part
        return result["acc"]

    pending = None
    for c0 in range(0, D_FF, chunk):
        c1 = min(c0 + chunk, D_FF)
        g = jnp.dot(h, wg_ref[:, c0:c1], preferred_element_type=_F32)
        yield
        u = jnp.dot(h, wu_ref[:, c0:c1], preferred_element_type=_F32)
        yield
        if pending is not None:
            acc = down(*pending)
            yield
        pending = (g, u, c0, c1)
        if not defer_down:
            acc = down(*pending)
            pending = None
            yield
    if pending is not None:
        acc = down(*pending)
        yield


def _swiglu_residual(x, gain, wg_ref, wu_ref, wd_ref):
    result = {}
    for _ in _ffn_dots(_rms_norm(x, gain).astype(_BF16), wg_ref, wu_ref, wd_ref, FFN1_CHUNK, True, result):
        pass
    return x + FFN_RES_SCALE * result["acc"]


def _rotate_head_pair(t, cos, sin_lo, sin_hi):
    return t * cos + pltpu.roll(t, LANES - ROT_HALF, 1) * sin_lo + pltpu.roll(t, ROT_HALF, 1) * sin_hi


def _rotary_coefficients(tile_cs, row_tab):
    ca, sa = tile_cs[:, :LANES], tile_cs[:, LANES:]
    tab = [row_tab[:, j * LANES:(j + 1) * LANES] for j in range(6)]
    return ca * tab[0] - sa * tab[1], sa * tab[2] + ca * tab[3], sa * tab[4] + ca * tab[5]


def _mixer_inputs(z_ref):
    return (z_ref.at[:, pl.ds(Z_BG, CONV_WIDTH)], z_ref.at[:, pl.ds(Z_CU, CONV_WIDTH)],
            z_ref.at[:, pl.ds(Z_Q, ATTN_WIDTH)], z_ref.at[:, pl.ds(Z_KV, 2 * KV_WIDTH)])


def _inproj_pieces(h, win_ref, tile_cs, row_tab, bg_ref, cu_ref, q_ref, kv_ref):
    def project(c0, width):
        return jnp.dot(h, win_ref[:, c0:c0 + width], preferred_element_type=_F32)

    cos, sin_lo, sin_hi = _rotary_coefficients(tile_cs, row_tab)
    q = project(3 * CONV_WIDTH, ATTN_WIDTH)
    for j in range(ATTN_WIDTH // LANES):
        q_ref[:, j * LANES:(j + 1) * LANES] = (
            _rotate_head_pair(q[:, j * LANES:(j + 1) * LANES], cos, sin_lo, sin_hi) * (HEAD_DIM ** -0.5)
        ).astype(_BF16)
    yield
    kv = project(3 * CONV_WIDTH + ATTN_WIDTH, 2 * KV_WIDTH)
    kv_ref[:, :KV_WIDTH] = _rotate_head_pair(kv[:, :KV_WIDTH], cos, sin_lo, sin_hi).astype(_BF16)
    kv_ref[:, KV_WIDTH:] = kv[:, KV_WIDTH:].astype(_BF16)
    yield
    bg_ref[...] = project(0, CONV_WIDTH).astype(_BF16)
    yield
    c_gate = project(CONV_WIDTH, CONV_WIDTH)
    yield
    cu_ref[...] = (c_gate * project(2 * CONV_WIDTH, CONV_WIDTH)).astype(_BF16)
    yield


def _cast_chunk_rows(rows, cols):
    fits = [r for r in range(BF16_SUBLANES, rows + 1, BF16_SUBLANES) if rows % r == 0 and r * cols * 4 <= CAST_CHUNK_BYTES]
    return max(fits)


class _CastStream:
    def __init__(self, src_hbm_ref, dst_ref, stage_ref, sem_ref):
        self.src, self.dst, self.stage, self.sem = src_hbm_ref, dst_ref, stage_ref, sem_ref
        self.n_slots, self.chunk = stage_ref.shape[:2]
        self.n_chunks = src_hbm_ref.shape[0] // self.chunk
        self.ahead = self.n_slots - 1

    def _copy(self, k):
        slot = k % self.n_slots
        return pltpu.make_async_copy(self.src.at[pl.ds(k * self.chunk, self.chunk)], self.stage.at[slot],
                                     self.sem.at[slot])

    def prime(self):
        for k in range(min(self.ahead, self.n_chunks)):
            self._copy(k).start()

    def drain(self):
        def body(k, carry):
            @pl.when(k + self.ahead < self.n_chunks)
            def _():
                self._copy(k + self.ahead).start()

            self._copy(k).wait()
            rows = pl.ds(pl.multiple_of(k * self.chunk, self.chunk), self.chunk)
            self.dst[rows, :] = self.stage[k % self.n_slots].astype(_BF16)
            return carry

        lax.fori_loop(0, self.n_chunks, body, 0)


def _ffn_inproj_kernel(x_ref, g1_ref, wg_hbm_ref, wu_hbm_ref, wd_hbm_ref, g2_ref, win_hbm_ref, tile_cs_ref, row_tab_ref,
                       wout_f32_ref, wg2_f32_ref, wu2_f32_ref, wd2_f32_ref,
                       x1_ref, z_ref, wout_ref, wg2_ref, wu2_ref, wd2_ref,
                       h_ref, wg_ref, wu_ref, wd_ref, win_ref, stage_ff_ref, stage_dn_ref, stage_in_ref,
                       sem_ff_ref, sem_dn_ref, sem_in_ref, *, n_tiles):
    for src_ref, dst_ref in ((wout_f32_ref, wout_ref), (wg2_f32_ref, wg2_ref), (wu2_f32_ref, wu2_ref),
                             (wd2_f32_ref, wd2_ref)):
        dst_ref[...] = src_ref[...].astype(_BF16)
    step = pl.program_id(0)

    @pl.when(step == 0)
    def _():
        gate = _CastStream(wg_hbm_ref, wg_ref, stage_ff_ref, sem_ff_ref)
        up = _CastStream(wu_hbm_ref, wu_ref, stage_ff_ref, sem_ff_ref)
        down = _CastStream(wd_hbm_ref, wd_ref, stage_dn_ref, sem_dn_ref)
        proj = _CastStream(win_hbm_ref, win_ref, stage_in_ref, sem_in_ref)
        gate.prime()
        down.prime()
        gate.drain()
        up.prime()
        down.drain()
        proj.prime()
        up.drain()
        proj.drain()

    def run(with_ffn):
        inproj = _inproj_pieces(h_ref[(step + 1) % 2], win_ref, tile_cs_ref[0], row_tab_ref[...],
                                *_mixer_inputs(z_ref))
        next(inproj)
        if with_ffn:
            x1 = _swiglu_residual(x_ref[...], g1_ref[...], wg_ref, wu_ref, wd_ref)
            x1_ref[...] = x1
            h_ref[step % 2] = _rms_norm(x1, g2_ref[...]).astype(_BF16)
        for _ in inproj:
            pass

    @pl.when(step == 0)
    def _():
        h_ref[1] = jnp.zeros(h_ref.shape[1:], h_ref.dtype)

    pl.when(step < n_tiles)(functools.partial(run, True))
    pl.when(step == n_tiles)(functools.partial(run, False))


def _conv_mixer(first_tile, bg_ref, cu_ref, cu_halo_ref, convw_ref):
    rows = cu_ref.shape[0]
    v = cu_ref[...].astype(_F32)
    halo = jnp.where(first_tile, 0.0, cu_halo_ref[...].astype(_F32))
    prev1 = halo[BF16_SUBLANES - 1:BF16_SUBLANES, :]
    prev2 = halo[BF16_SUBLANES - 2:BF16_SUBLANES - 1, :]
    row = lax.broadcasted_iota(jnp.int32, (rows, 1), 0)
    v1 = jnp.where(row == 0, prev1, pltpu.roll(v, 1, 0))
    v2 = jnp.where(row == 0, prev2, jnp.where(row == 1, prev1, pltpu.roll(v, 2, 0)))
    w = convw_ref[...]
    return bg_ref[...].astype(_F32) * (w[0:1, :] * v2 + w[1:2, :] * v1 + w[2:3, :] * v)


def _attention_pieces(first_tile, sinks_ref, q_ref, kv_ref, kv_halo_ref, result):
    rows = q_ref.shape[0]
    kv_ext = jnp.concatenate([kv_halo_ref[...], kv_ref[...]], axis=0)
    stacked = Q_PER_KV * BLOCK
    row = lax.broadcasted_iota(jnp.int32, (stacked, 2 * BLOCK), 0)
    qi = row % BLOCK
    kj = lax.broadcasted_iota(jnp.int32, (stacked, 2 * BLOCK), 1)
    in_window = (kj > qi) & (kj <= qi + BLOCK)
    first_mask = in_window & ((kj >= BLOCK) | jnp.logical_not(first_tile))
    group = lax.broadcasted_iota(jnp.int32, (stacked, 1), 0) // BLOCK

    def sink_column(hkv):
        col = jnp.full((stacked, 1), sinks_ref[hkv * Q_PER_KV], _F32)
        for g in range(1, Q_PER_KV):
            col = jnp.where(group == g, sinks_ref[hkv * Q_PER_KV + g], col)
        return col

    sink_columns = [sink_column(hkv) for hkv in range(N_KV_HEADS)]

    def scores(blk, hkv):
        r0 = blk * BLOCK
        k = kv_ext[r0:r0 + 2 * BLOCK, hkv * HEAD_DIM:(hkv + 1) * HEAD_DIM]
        q = jnp.concatenate([q_ref[r0:r0 + BLOCK, hq * HEAD_DIM:(hq + 1) * HEAD_DIM]
                             for hq in range(hkv * Q_PER_KV, (hkv + 1) * Q_PER_KV)], axis=0)
        return lax.dot_general(q, k, (((1,), (1,)), ((), ())), preferred_element_type=_F32)

    def weighted_values(blk, hkv, s):
        r0 = blk * BLOCK
        vv = kv_ext[r0:r0 + 2 * BLOCK, KV_WIDTH + hkv * HEAD_DIM:KV_WIDTH + (hkv + 1) * HEAD_DIM]
        sink = sink_columns[hkv]
        s = jnp.where(in_window if blk > 0 else first_mask, s, MASK_VALUE)
        m = jnp.maximum(jnp.max(s, axis=-1, keepdims=True), sink)
        p = jnp.exp(s - m)
        denom = jnp.sum(p, axis=-1, keepdims=True) + jnp.exp(sink - m)
        o = jnp.dot(p.astype(_BF16), vv, preferred_element_type=_F32) / denom
        return [o[g * BLOCK:(g + 1) * BLOCK] for g in range(Q_PER_KV)]

    pieces = [(blk, hkv) for blk in range(rows // BLOCK) for hkv in range(N_KV_HEADS)]
    pending = [scores(*piece) for piece in pieces[:SCORE_LOOKAHEAD]]
    yield
    blocks, heads = [], []
    for i, piece in enumerate(pieces):
        heads.extend(weighted_values(*piece, pending.pop(0)))
        if len(heads) == N_Q_HEADS:
            blocks.append(jnp.concatenate(heads, axis=1))
            heads = []
        if i + SCORE_LOOKAHEAD < len(pieces):
            pending.append(scores(*pieces[i + SCORE_LOOKAHEAD]))
        if i + 1 < len(pieces):
            yield
    result["y_attn"] = jnp.concatenate(blocks, axis=0)


def _mixer_ffn_kernel(sinks_ref, x1_ref, z_ref, cu_halo_ref, kv_halo_ref, convw_ref, wout_ref,
                      g_ref, wg_ref, wu_ref, wd_ref, gf_ref, out_ref, x2_ref, h_ref, *, tiles_per_seq):
    step = pl.program_id(0)

    @pl.when(step == 0)
    def _():
        x2_ref[1] = jnp.zeros(x2_ref.shape[1:], x2_ref.dtype)
        h_ref[1] = jnp.zeros(h_ref.shape[1:], h_ref.dtype)

    bg_ref, cu_ref, q_ref, kv_ref = _mixer_inputs(z_ref)
    read_slot = (step + 1) % 2
    write_slot = step % 2
    first_tile = step % tiles_per_seq == 0
    ffn, att = {}, {}
    ffn_dots = _ffn_dots(h_ref[read_slot], wg_ref, wu_ref, wd_ref, FFN2_CHUNK, False, ffn)
    next(ffn_dots)
    y_conv = _conv_mixer(first_tile, bg_ref, cu_ref, cu_halo_ref, convw_ref)
    for _ in _attention_pieces(first_tile, sinks_ref, q_ref, kv_ref, kv_halo_ref, att):
        next(ffn_dots, None)
    y = jnp.concatenate([y_conv, att["y_attn"]], axis=1).astype(_BF16)
    x2 = x1_ref[...] + jnp.dot(y, wout_ref[...], preferred_element_type=_F32)
    x2_ref[write_slot] = x2
    h_ref[write_slot] = _rms_norm(x2, g_ref[...]).astype(_BF16)
    for _ in ffn_dots:
        pass
    x = x2_ref[read_slot] + FFN_RES_SCALE * ffn["acc"]
    out_ref[...] = _rms_norm(x, gf_ref[...])


def _resident(shape):
    return pl.BlockSpec(shape, lambda *_: (0,) * len(shape), pipeline_mode=pl.Buffered(1))


def _rotary_tables(seq, tile):
    inv_freq = ROPE_THETA ** (-jnp.arange(0, ROT_DIM, 2, dtype=_F32) / ROT_DIM)
    dim = jnp.arange(LANES) % HEAD_DIM
    freq = jnp.where(dim < ROT_DIM, inv_freq[dim % ROT_HALF], 0.0)
    lo = (dim < ROT_HALF).astype(_F32)
    hi = ((dim >= ROT_HALF) & (dim < ROT_DIM)).astype(_F32)
    start = (jnp.arange(seq // tile) * tile).astype(_F32)[:, None] * freq[None, :]
    row = jnp.arange(tile, dtype=_F32)[:, None] * freq[None, :]
    tile_cs = jnp.concatenate([jnp.cos(start), jnp.sin(start)], axis=1).reshape(seq // tile, 1, 2 * LANES)
    cb, sb = jnp.cos(row), jnp.sin(row)
    return tile_cs, jnp.concatenate([cb, sb, -cb * lo, -sb * lo, cb * hi, sb * hi], axis=1)


_FFN_WEIGHT_SPECS = [_resident((D_MODEL, D_FF)), _resident((D_MODEL, D_FF)), _resident((D_FF, D_MODEL))]
_PARAMS = functools.partial(pltpu.CompilerParams, vmem_limit_bytes=VMEM_LIMIT_BYTES)


def _cast_block_rows(rows, steps):
    return next(r for r in range(BF16_SUBLANES, rows + 1, BF16_SUBLANES) if rows % r == 0 and r * steps >= rows)


def _ffn_inproj_call(xt, gain1, wg, wu, wd, gain2, win, seq, later_weights):
    tokens = xt.shape[0]
    n_tiles = tokens // FFN_TILE
    seq_tiles = seq // FFN_TILE
    hbm = pl.BlockSpec(memory_space=pl.ANY)
    cast_specs, cast_shapes = [], []
    for w in later_weights:
        rows, cols = w.shape
        block_rows = _cast_block_rows(rows, n_tiles)
        index_map = functools.partial(lambda s, last: (jnp.minimum(s, last), 0), last=rows // block_rows - 1)
        cast_specs.append(pl.BlockSpec((block_rows, cols), index_map))
        cast_shapes.append(jax.ShapeDtypeStruct(w.shape, _BF16))

    def ffn_spec(width):
        return pl.BlockSpec((FFN_TILE, width), lambda s: (jnp.minimum(s, n_tiles - 1), 0))

    def proj_spec(width):
        return pl.BlockSpec((FFN_TILE, width), lambda s: (jnp.maximum(s - 1, 0), 0))

    return pl.pallas_call(
        functools.partial(_ffn_inproj_kernel, n_tiles=n_tiles),
        grid=(n_tiles + 1,),
        in_specs=[ffn_spec(D_MODEL), _resident((1, D_MODEL)), hbm, hbm, hbm,
                  _resident((1, D_MODEL)), hbm,
                  pl.BlockSpec((1, 1, 2 * LANES), lambda s: (jnp.maximum(s - 1, 0) % seq_tiles, 0, 0)),
                  _resident((FFN_TILE, 6 * LANES)), *cast_specs],
        out_specs=[ffn_spec(D_MODEL), proj_spec(Z_WIDTH), *cast_specs],
        out_shape=[jax.ShapeDtypeStruct((tokens, D_MODEL), _F32),
                   jax.ShapeDtypeStruct((tokens, Z_WIDTH), _BF16), *cast_shapes],
        scratch_shapes=[pltpu.VMEM((2, FFN_TILE, D_MODEL), _BF16),
                        pltpu.VMEM(wg.shape, _BF16), pltpu.VMEM(wu.shape, _BF16), pltpu.VMEM(wd.shape, _BF16),
                        pltpu.VMEM(win.shape, _BF16),
                        *[pltpu.VMEM((CAST_SLOTS, _cast_chunk_rows(*w.shape), w.shape[1]), _F32)
                          for w in (wg, wd, win)],
                        *[pltpu.SemaphoreType.DMA((CAST_SLOTS,))] * 3],
        compiler_params=_PARAMS(dimension_semantics=("arbitrary",)),
        name="ffn1_inproj",
    )(xt, gain1, wg, wu, wd, gain2, win, *_rotary_tables(seq, FFN_TILE), *later_weights)


def _mixer_ffn_call(sinks, x1, z, conv_w, wout, gain, wg, wu, wd, final_gain, seq):
    tokens = x1.shape[0]
    n_tiles = tokens // MIX_TILE
    halo_per_tile = MIX_TILE // BF16_SUBLANES
    blocks_per_tile = MIX_TILE // BLOCK

    def mix_tile(s):
        return jnp.minimum(s, n_tiles - 1)

    def tile_spec(width):
        return pl.BlockSpec((MIX_TILE, width), lambda s, sinks: (mix_tile(s), 0))

    return pl.pallas_call(
        functools.partial(_mixer_ffn_kernel, tiles_per_seq=seq // MIX_TILE),
        grid_spec=pltpu.PrefetchScalarGridSpec(
            num_scalar_prefetch=1,
            grid=(n_tiles + 1,),
            in_specs=[
                tile_spec(D_MODEL), tile_spec(Z_WIDTH),
                pl.BlockSpec((BF16_SUBLANES, CONV_WIDTH),
                             lambda s, sinks: (jnp.maximum(mix_tile(s) * halo_per_tile - 1, 0), Z_CU // CONV_WIDTH)),
                pl.BlockSpec((BLOCK, 2 * KV_WIDTH),
                             lambda s, sinks: (jnp.maximum(mix_tile(s) * blocks_per_tile - 1, 0),
                                               Z_KV // (2 * KV_WIDTH))),
                _resident((CONV_KERNEL, CONV_WIDTH)), _resident((MIX_WIDTH, D_MODEL)),
                _resident((1, D_MODEL)), *_FFN_WEIGHT_SPECS, _resident((1, D_MODEL)),
            ],
            out_specs=pl.BlockSpec((MIX_TILE, D_MODEL), lambda s, sinks: (jnp.maximum(s - 1, 0), 0)),
            scratch_shapes=[pltpu.VMEM((2, MIX_TILE, D_MODEL), _F32), pltpu.VMEM((2, MIX_TILE, D_MODEL), _BF16)],
        ),
        out_shape=jax.ShapeDtypeStruct((tokens, D_MODEL), _F32),
        compiler_params=_PARAMS(dimension_semantics=("arbitrary",)),
        name="mixer_ffn2",
    )(sinks, x1, z, z, z, conv_w, wout, gain, wg, wu, wd, final_gain)


def kernel(x, ffn1_norm, ffn1_w_gate, ffn1_w_up, ffn1_w_down, mix_norm, w_in, conv_w, attn_sinks, w_out,
           ffn2_norm, ffn2_w_gate, ffn2_w_up, ffn2_w_down, final_norm):
    batch, seq, d_model = x.shape
    assert d_model == D_MODEL and ffn1_norm.shape[0] == 1
    assert seq % MIX_TILE == 0 and seq % FFN_TILE == 0
    xt = x.reshape(batch * seq, D_MODEL)
    x1, z, wout, wg2, wu2, wd2 = _ffn_inproj_call(
        xt, ffn1_norm, ffn1_w_gate[0], ffn1_w_up[0], ffn1_w_down[0], mix_norm, w_in[0], seq,
        (w_out[0], ffn2_w_gate[0], ffn2_w_up[0], ffn2_w_down[0]))
    out = _mixer_ffn_call(
        attn_sinks[0], x1, z, conv_w[0], wout, ffn2_norm, wg2, wu2, wd2,
        final_norm.reshape(1, D_MODEL), seq)
    return out.reshape(batch, seq, D_MODEL)
```
